```python
import math
import jax, jax.numpy as jnp
from jax import lax
import numpy as np

D_MODEL = 2048
BATCH = 16
SEQ = 2048
DEPTH = 1

CHUNK = 64
Q_BLOCK = 128
MAX_OFFSET = 65536

MLA_HEADS = 8
Q_LORA = 512
KV_LORA = 256
NOPE_DIM = 128
ROPE_DIM = 64
V_DIM = 128
ROPE_THETA = 10000.0

FOX_HEADS = 8
FOX_HEAD_DIM = 128

D_FF = 5632
CONV_WIDTH = 3

N_BRANCHES = 2
EPS = 1e-6
NEG_INF = -1e30

SPLITS = (
    Q_LORA,
    KV_LORA,
    ROPE_DIM,
    FOX_HEADS * FOX_HEAD_DIM,
    FOX_HEADS * FOX_HEAD_DIM,
    FOX_HEADS * FOX_HEAD_DIM,
    FOX_HEADS,
    N_BRANCHES * D_MODEL,
)
D_IN = sum(SPLITS)

kernel_name = "hybrid_mla_fox_convffn_block"


def rmsnorm(x, g):
    xf = x.astype(jnp.float32)
    y = xf * lax.rsqrt(jnp.mean(xf * xf, axis=-1, keepdims=True) + EPS)
    return (y * g.astype(jnp.float32)).astype(x.dtype)


def rope(x, cos, sin):
    half = x.shape[-1] // 2
    x1, x2 = x[..., :half], x[..., half:]
    return jnp.concatenate([x1 * cos - x2 * sin, x2 * cos + x1 * sin], axis=-1)


def rope_tables(positions, dtype):
    inv_freq = 1.0 / (ROPE_THETA ** (jnp.arange(0, ROPE_DIM, 2, dtype=jnp.float32) / ROPE_DIM))
    ang = positions.astype(jnp.float32)[..., None] * inv_freq
    return jnp.cos(ang).astype(dtype), jnp.sin(ang).astype(dtype)


def block_attention(q, k, v, causal_unit, log_decay=None):
    S = q.shape[1]
    scale = q.shape[-1] ** -0.5
    pos = jnp.arange(S)
    outs = []
    for i in range(S // Q_BLOCK):
        start, end = i * Q_BLOCK, (i + 1) * Q_BLOCK
        s = jnp.einsum('bqhd,bkhd->bhqk', q[:, start:end], k[:, :end]).astype(jnp.float32) * scale
        if log_decay is not None:
            s = s + log_decay[:, :, start:end, None] - log_decay[:, :, None, :end]
        visible = (pos[None, :end] // causal_unit) <= (pos[start:end, None] // causal_unit)
        s = jnp.where(visible, s, NEG_INF)
        p = jax.nn.softmax(s, axis=-1)
        outs.append(jnp.einsum('bhqk,bkhd->bqhd', p.astype(v.dtype), v[:, :end]))
    return jnp.concatenate(outs, axis=1)


def setup_inputs(seed: int = 0) -> dict:
    key = jax.random.key(seed)
    ks = jax.random.split(key, 24)
    f32 = jnp.float32
    nrm = lambda k, shape, fan_in: jax.random.normal(k, shape, f32) * (fan_in ** -0.5)
    gain = lambda k, n: 1.0 + 0.05 * jax.random.normal(k, (n,), f32)
    offset = jax.random.randint(ks[1], (BATCH, 1), 0, MAX_OFFSET, dtype=jnp.int32)
    positions = offset + jnp.arange(SEQ, dtype=jnp.int32)[None, :]
    return {
        "x": jax.random.normal(ks[0], (BATCH, SEQ, D_MODEL), f32),
        "positions": positions,
        "pre_mix_norm": gain(ks[2], D_MODEL),
        "w_in": nrm(ks[3], (D_MODEL, D_IN), D_MODEL),
        "q_a_norm": gain(ks[4], Q_LORA),
        "w_uq": nrm(ks[5], (Q_LORA, MLA_HEADS * (NOPE_DIM + ROPE_DIM)), Q_LORA),
        "kv_a_norm": gain(ks[6], KV_LORA),
        "w_ukv": nrm(ks[7], (KV_LORA, MLA_HEADS * (NOPE_DIM + V_DIM)), KV_LORA),
        "b_forget": jax.random.uniform(ks[8], (FOX_HEADS,), f32, 1.0, 4.0),
        "b_gate": 0.02 * jax.random.normal(ks[9], (N_BRANCHES * D_MODEL,), f32),
        "w_branch_mla": nrm(ks[10], (MLA_HEADS * V_DIM, D_MODEL), MLA_HEADS * V_DIM),
        "w_branch_fox": nrm(ks[11], (FOX_HEADS * FOX_HEAD_DIM, D_MODEL), FOX_HEADS * FOX_HEAD_DIM),
        "w_out": nrm(ks[12], (D_MODEL, D_MODEL), D_MODEL),
        "post_mix_norm": gain(ks[13], D_MODEL),
        "pre_ffn_norm": gain(ks[14], D_MODEL),
        "w_up": nrm(ks[15], (D_MODEL, 2 * D_FF), D_MODEL),
        "conv_w": nrm(ks[16], (CONV_WIDTH, 2 * D_FF), CONV_WIDTH),
        "conv_b": 0.02 * jax.random.normal(ks[17], (2 * D_FF,), f32),
        "w_down": nrm(ks[18], (D_FF, D_MODEL), D_FF),
        "post_ffn_norm": gain(ks[19], D_MODEL),
    }


def reference(x, positions, pre_mix_norm, w_in, q_a_norm, w_uq, kv_a_norm, w_ukv,
              b_forget, b_gate, w_branch_mla, w_branch_fox, w_out, post_mix_norm,
              pre_ffn_norm, w_up, conv_w, conv_b, w_down, post_ffn_norm):
    B, S, _ = x.shape
    cos, sin = rope_tables(positions, x.dtype)
    for _layer in range(DEPTH):
        h = rmsnorm(x, pre_mix_norm)
        proj = h @ w_in
        cuts = np.cumsum(SPLITS)[:-1].tolist()
        q_lat, kv_lat, k_pe, fq, fk, fv, f_logit, g_logit = jnp.split(proj, cuts, axis=-1)

        q = (rmsnorm(q_lat, q_a_norm) @ w_uq).reshape(B, S, MLA_HEADS, NOPE_DIM + ROPE_DIM)
        q_nope, q_pe = q[..., :NOPE_DIM], q[..., NOPE_DIM:]
        q_pe = rope(q_pe, cos[:, :, None, :], sin[:, :, None, :])
        kv = (rmsnorm(kv_lat, kv_a_norm) @ w_ukv).reshape(B, S, MLA_HEADS, NOPE_DIM + V_DIM)
        k_nope, v_mla = kv[..., :NOPE_DIM], kv[..., NOPE_DIM:]
        k_pe = rope(k_pe, cos, sin)[:, :, None, :]
        q_mla = jnp.concatenate([q_nope, q_pe], axis=-1)
        k_mla = jnp.concatenate([k_nope, jnp.broadcast_to(k_pe, (B, S, MLA_HEADS, ROPE_DIM))], axis=-1)
        o_mla = block_attention(q_mla, k_mla, v_mla, CHUNK).reshape(B, S, MLA_HEADS * V_DIM)

        log_f = jax.nn.log_sigmoid(f_logit.astype(jnp.float32) + b_forget.astype(jnp.float32))
        c = jnp.transpose(jnp.cumsum(log_f, axis=1), (0, 2, 1))
        shp = (B, S, FOX_HEADS, FOX_HEAD_DIM)
        o_fox = block_attention(fq.reshape(shp), fk.reshape(shp), fv.reshape(shp), 1, c)
        o_fox = o_fox.reshape(B, S, FOX_HEADS * FOX_HEAD_DIM)

        gates = jax.nn.sigmoid((g_logit + b_gate).astype(jnp.float32)).astype(x.dtype)
        g_mla, g_fox = gates[..., :D_MODEL], gates[..., D_MODEL:]
        merged = g_mla * (o_mla @ w_branch_mla) + g_fox * (o_fox @ w_branch_fox)
        x = x + rmsnorm(merged @ w_out, post_mix_norm)

        h2 = rmsnorm(x, pre_ffn_norm)
        u = h2 @ w_up
        u_pad = jnp.pad(u, ((0, 0), (CONV_WIDTH - 1, 0), (0, 0)))
        u = sum(conv_w[j] * u_pad[:, j:j + S] for j in range(CONV_WIDTH)) + conv_b
        gate, val = u[..., :D_FF], u[..., D_FF:]
        ff = (jax.nn.gelu(gate, approximate=True) * val) @ w_down
        x = x + rmsnorm(ff, post_ffn_norm)
    return x
```

```python
import functools

import jax
import jax.numpy as jnp
from jax import lax
from jax.experimental import pallas as pl
from jax.experimental.pallas import tpu as pltpu

D_MODEL = 2048
CHUNK = 64
MLA_HEADS = 8
Q_LORA = 512
KV_LORA = 256
NOPE_DIM = 128
ROPE_DIM = 64
V_DIM = 128
ROPE_THETA = 10000.0
FOX_HEADS = 8
FOX_HEAD_DIM = 128
D_FF = 5632
CONV_WIDTH = 3
EPS = 1e-6
NEG_INF = -1e30

LANES = 128
MLA_QK_PAD = 256
SMALL_COLS = 1024
MAIN_COLS = 2 * D_MODEL + 3 * FOX_HEADS * FOX_HEAD_DIM
VMEM_LIMIT = 56 * 1024 * 1024

F32 = jnp.float32
BF16 = jnp.bfloat16


def _rms(x, g):
    return x * lax.rsqrt(jnp.mean(x * x, axis=-1, keepdims=True) + EPS) * g


def _dot(a, b):
    return jnp.dot(a, b, preferred_element_type=F32)


def _params(n_axes):
    return pltpu.CompilerParams(dimension_semantics=("arbitrary",) * n_axes,
                                vmem_limit_bytes=VMEM_LIMIT)


def _resident(shape):
    return pl.BlockSpec(shape, lambda *_: (0,) * len(shape), pipeline_mode=pl.Buffered(1))


def _proj_main_kernel(x_ref, g_ref, w_ref, o_ref, h_sc):
    @pl.when(pl.program_id(1) == 0)
    def _():
        h_sc[...] = _rms(x_ref[...], g_ref[...]).astype(BF16)

    o_ref[...] = _dot(h_sc[...], w_ref[...]).astype(o_ref.dtype)


def _proj_main(x2, g_pre, w_main, tm, tn):
    n_tok = x2.shape[0]
    return pl.pallas_call(
        _proj_main_kernel,
        grid=(n_tok // tm, MAIN_COLS // tn),
        in_specs=[pl.BlockSpec((tm, D_MODEL), lambda i, j: (i, 0)),
                  pl.BlockSpec((1, D_MODEL), lambda i, j: (0, 0)),
                  pl.BlockSpec((D_MODEL, tn), lambda i, j: (0, j))],
        out_specs=pl.BlockSpec((tm, tn), lambda i, j: (i, j)),
        out_shape=jax.ShapeDtypeStruct((n_tok, MAIN_COLS), BF16),
        scratch_shapes=[pltpu.VMEM((tm, D_MODEL), BF16)],
        compiler_params=_params(2),
        name="proj_main",
    )(x2, g_pre, w_main)


def _cumsum_rows(v):
    n = v.shape[0]
    rows = lax.broadcasted_iota(jnp.int32, v.shape, 0)
    shift = 1
    while shift < n:
        v = v + jnp.where(rows >= shift, pltpu.roll(v, shift, axis=0), 0.0)
        shift *= 2
    return v


def _mla_prep_kernel(x_ref, pos_ref, g_ref, ws_ref, gq_ref, wuq_ref, gkv_ref, wukv_ref, bf_ref,
                     invf_ref, q_ref, k_ref, v_ref, c_ref, carry_sc, *, blocks_per_seq):
    tm = x_ref.shape[0]
    h = _rms(x_ref[...], g_ref[...]).astype(BF16)
    small = _dot(h, ws_ref[...])

    lane = lax.broadcasted_iota(jnp.int32, (tm, LANES), 1)
    ang = pos_ref[...].astype(F32) * invf_ref[...]
    sin_part = jnp.where(lane < 96, -jnp.sin(ang), jnp.sin(ang))
    rot = jnp.where(lane < 64, jnp.cos(ang), sin_part)

    def rope(tile):
        p = tile * rot
        return p + pltpu.roll(p, 64, axis=1)

    qn = _rms(small[:, :Q_LORA], gq_ref[...]).astype(BF16)
    q_all = _dot(qn, wuq_ref[...])
    for hd in range(MLA_HEADS):
        c0 = hd * MLA_QK_PAD
        q_ref[:, c0:c0 + LANES] = q_all[:, c0:c0 + LANES].astype(BF16)
        q_ref[:, c0 + LANES:c0 + 2 * LANES] = rope(q_all[:, c0 + LANES:c0 + 2 * LANES]).astype(BF16)

    kvn = _rms(small[:, Q_LORA:Q_LORA + KV_LORA], gkv_ref[...]).astype(BF16)
    kv = _dot(kvn, wukv_ref[...])
    kpe_col = Q_LORA + KV_LORA
    kpe = jnp.where(lane < ROPE_DIM, rope(small[:, kpe_col:kpe_col + LANES]), 0.0).astype(BF16)
    for hd in range(MLA_HEADS):
        c0 = hd * MLA_QK_PAD
        k_ref[:, c0:c0 + LANES] = kv[:, hd * NOPE_DIM:(hd + 1) * NOPE_DIM].astype(BF16)
        k_ref[:, c0 + LANES:c0 + 2 * LANES] = kpe
    v_ref[...] = kv[:, MLA_HEADS * NOPE_DIM:].astype(BF16)

    @pl.when(pl.program_id(0) % blocks_per_seq == 0)
    def _():
        carry_sc[...] = jnp.zeros_like(carry_sc)

    z = small[:, kpe_col + LANES:kpe_col + 2 * LANES] + bf_ref[...]
    log_f = jnp.minimum(z, 0.0) - jnp.log1p(jnp.exp(-jnp.abs(z)))
    c_blk = _cumsum_rows(log_f) + carry_sc[0:1, :]
    carry_sc[...] = jnp.broadcast_to(c_blk[tm - 1:tm, :], carry_sc.shape)
    c_ref[0] = c_blk.T[:FOX_HEADS, :]


def _mla_prep(x2, pos2, g_pre, w_small, gq, w_uq, gkv, w_ukv, b_forget_pad, invf, batch, seq, tm):
    n_tok = x2.shape[0]
    bps = seq // tm
    row = lambda i: (i, 0)
    return pl.pallas_call(
        functools.partial(_mla_prep_kernel, blocks_per_seq=bps),
        grid=(n_tok // tm,),
        in_specs=[pl.BlockSpec((tm, D_MODEL), row),
                  pl.BlockSpec((tm, 1), row),
                  _resident((1, D_MODEL)),
                  _resident((D_MODEL, SMALL_COLS)),
                  _resident((1, Q_LORA)),
                  _resident((Q_LORA, MLA_HEADS * MLA_QK_PAD)),
                  _resident((1, KV_LORA)),
                  _resident((KV_LORA, MLA_HEADS * (NOPE_DIM + V_DIM))),
                  _resident((1, LANES)),
                  _resident((1, LANES))],
        out_specs=[pl.BlockSpec((tm, MLA_HEADS * MLA_QK_PAD), row),
                   pl.BlockSpec((tm, MLA_HEADS * MLA_QK_PAD), row),
                   pl.BlockSpec((tm, MLA_HEADS * V_DIM), row),
                   pl.BlockSpec((1, FOX_HEADS, tm), lambda i: (i // bps, 0, i % bps))],
        out_shape=[jax.ShapeDtypeStruct((n_tok, MLA_HEADS * MLA_QK_PAD), BF16),
                   jax.ShapeDtypeStruct((n_tok, MLA_HEADS * MLA_QK_PAD), BF16),
                   jax.ShapeDtypeStruct((n_tok, MLA_HEADS * V_DIM), BF16),
                   jax.ShapeDtypeStruct((batch, FOX_HEADS, seq), F32)],
        scratch_shapes=[pltpu.VMEM((8, LANES), F32)],
        compiler_params=_params(1),
        name="mla_prep",
    )(x2, pos2, g_pre, w_small, gq, w_uq, gkv, w_ukv, b_forget_pad, invf)


def _attn_kernel(*refs, blk, unit, use_decay):
    if use_decay:
        q_ref, k_ref, v_ref, c_ref, o_ref = refs
    else:
        q_ref, k_ref, v_ref, o_ref = refs
        c_ref = None
    seq = q_ref.shape[1]
    dv = v_ref.shape[2]
    row = lax.broadcasted_iota(jnp.int32, (blk, blk), 0)
    col = lax.broadcasted_iota(jnp.int32, (blk, blk), 1)
    diag_mask = (col // unit) <= (row // unit)

    for qi in range(seq // blk):
        q = q_ref[0, qi * blk:(qi + 1) * blk, :]

        def step(start, carry, mask):
            m, l, acc = carry
            k = k_ref[0, pl.ds(start, blk), :]
            v = v_ref[0, pl.ds(start, blk), :]
            s = lax.dot_general(q, k, (((1,), (1,)), ((), ())), preferred_element_type=F32)
            if use_decay:
                s = s - c_ref[0, 0, :, pl.ds(start, blk)]
            if mask is not None:
                s = jnp.where(mask, s, NEG_INF)
            m_new = jnp.maximum(m, jnp.max(s, axis=-1, keepdims=True))
            alpha = jnp.exp(m - m_new)
            p = jnp.exp(s - m_new)
            l = alpha * l + jnp.sum(p, axis=-1, keepdims=True)
            acc = alpha * acc + _dot(p.astype(BF16), v)
            return m_new, l, acc

        carry = (jnp.full((blk, 1), NEG_INF, F32), jnp.zeros((blk, 1), F32),
                 jnp.zeros((blk, dv), F32))
        if qi > 0:
            carry = lax.fori_loop(
                0, qi, lambda ki, c: step(pl.multiple_of(ki * blk, blk), c, None), carry)
        _, l, acc = step(qi * blk, carry, diag_mask)
        o_ref[0, qi * blk:(qi + 1) * blk, :] = (acc / l).astype(o_ref.dtype)


def _attention(q_arr, k_arr, v_arr, c_arr, *, dk, q_blk0, k_blk0, v_blk0, heads, unit, blk, name):
    batch, seq, _ = q_arr.shape
    dv = V_DIM
    in_specs = [pl.BlockSpec((1, seq, dk), lambda b, h: (b, 0, q_blk0 + h)),
                pl.BlockSpec((1, seq, dk), lambda b, h: (b, 0, k_blk0 + h)),
                pl.BlockSpec((1, seq, dv), lambda b, h: (b, 0, v_blk0 + h))]
    args = [q_arr, k_arr, v_arr]
    if c_arr is not None:
        in_specs.append(pl.BlockSpec((1, 1, 1, seq), lambda b, h: (b, h, 0, 0)))
        args.append(c_arr)
    return pl.pallas_call(
        functools.partial(_attn_kernel, blk=blk, unit=unit, use_decay=c_arr is not None),
        grid=(batch, heads),
        in_specs=in_specs,
        out_specs=pl.BlockSpec((1, seq, dv), lambda b, h: (b, 0, h)),
        out_shape=jax.ShapeDtypeStruct((batch, seq, heads * dv), BF16),
        compiler_params=_params(2),
        name=name,
    )(*args)


def _merge_kernel(x_ref, om_ref, of_ref, gm_ref, gf_ref, bg_ref, wm_ref, wf_ref, wo_ref, gn_ref,
                  o_ref):
    g_mla = jax.nn.sigmoid(gm_ref[...].astype(F32) + bg_ref[:, :D_MODEL])
    g_fox = jax.nn.sigmoid(gf_ref[...].astype(F32) + bg_ref[:, D_MODEL:])
    merged = g_mla * _dot(om_ref[...], wm_ref[...]) + g_fox * _dot(of_ref[...], wf_ref[...])
    y = _dot(merged.astype(BF16), wo_ref[...])
    o_ref[...] = x_ref[...] + _rms(y, gn_ref[...])


def _merge(x2, o_mla, o_fox, proj, b_gate, w_bm, w_bf, w_out, g_post, tm):
    n_tok = x2.shape[0]
    hv = MLA_HEADS * V_DIM
    row = lambda i: (i, 0)
    return pl.pallas_call(
        _merge_kernel,
        grid=(n_tok // tm,),
        in_specs=[pl.BlockSpec((tm, D_MODEL), row),
                  pl.BlockSpec((tm, hv), row),
                  pl.BlockSpec((tm, hv), row),
                  pl.BlockSpec((tm, D_MODEL), lambda i: (i, 0)),
                  pl.BlockSpec((tm, D_MODEL), lambda i: (i, 1)),
                  _resident((1, 2 * D_MODEL)),
                  _resident((hv, D_MODEL)),
                  _resident((hv, D_MODEL)),
                  _resident((D_MODEL, D_MODEL)),
                  _resident((1, D_MODEL))],
        out_specs=pl.BlockSpec((tm, D_MODEL), row),
        out_shape=jax.ShapeDtypeStruct((n_tok, D_MODEL), F32),
        compiler_params=_params(1),
        name="merge",
    )(x2, o_mla, o_fox, proj, proj, b_gate, w_bm, w_bf, w_out, g_post)


def _ffn_kernel(x_ref, gpre_ref, wg_ref, wv_ref, cwg_ref, cwv_ref, cbg_ref, cbv_ref, wd_ref,
                gpost_ref, o_ref, h_sc, acc_sc, halo_g, halo_v, *, blocks_per_seq):
    i = pl.program_id(0)
    k = pl.program_id(1)
    tm = x_ref.shape[0]

    @pl.when(k == 0)
    def _():
        h_sc[...] = _rms(x_ref[...], gpre_ref[...]).astype(BF16)
        acc_sc[...] = jnp.zeros_like(acc_sc)

    @pl.when(i % blocks_per_seq == 0)
    def _():
        halo_g[k] = jnp.zeros(halo_g.shape[1:], F32)
        halo_v[k] = jnp.zeros(halo_v.shape[1:], F32)

    rows = lax.broadcasted_iota(jnp.int32, (tm, 1), 0)

    def conv(u, halo, cw_ref, cb_ref):
        prev = halo[k]
        halo[k] = u[tm - 8:tm, :]
        p1, p2 = prev[7:8, :], prev[6:7, :]
        um1 = jnp.where(rows == 0, p1, pltpu.roll(u, 1, axis=0))
        um2 = jnp.where(rows == 0, p2, jnp.where(rows == 1, p1, pltpu.roll(u, 2, axis=0)))
        return cw_ref[0:1, :] * um2 + cw_ref[1:2, :] * um1 + cw_ref[2:3, :] * u + cb_ref[...]

    h = h_sc[...]
    gate = conv(_dot(h, wg_ref[...]), halo_g, cwg_ref, cbg_ref)
    val = conv(_dot(h, wv_ref[...]), halo_v, cwv_ref, cbv_ref)
    act = (jax.nn.gelu(gate, approximate=True) * val).astype(BF16)
    acc_sc[...] += _dot(act, wd_ref[...])

    @pl.when(k == pl.num_programs(1) - 1)
    def _():
        o_ref[...] = x_ref[...] + _rms(acc_sc[...], gpost_ref[...])


def _ffn(x1, g_pre, w_up, conv_w, conv_b, w_down, g_post, seq, tm, tc):
    n_tok = x1.shape[0]
    nk = D_FF // tc
    bps = seq // tm
    return pl.pallas_call(
        functools.partial(_ffn_kernel, blocks_per_seq=bps),
        grid=(n_tok // tm, nk),
        in_specs=[pl.BlockSpec((tm, D_MODEL), lambda i, k: (i, 0)),
                  pl.BlockSpec((1, D_MODEL), lambda i, k: (0, 0)),
                  pl.BlockSpec((D_MODEL, tc), lambda i, k: (0, k)),
                  pl.BlockSpec((D_MODEL, tc), lambda i, k: (0, nk + k)),
                  pl.BlockSpec((CONV_WIDTH, tc), lambda i, k: (0, k)),
                  pl.BlockSpec((CONV_WIDTH, tc), lambda i, k: (0, nk + k)),
                  pl.BlockSpec((1, tc), lambda i, k: (0, k)),
                  pl.BlockSpec((1, tc), lambda i, k: (0, nk + k)),
                  pl.BlockSpec((tc, D_MODEL), lambda i, k: (k, 0)),
                  pl.BlockSpec((1, D_MODEL), lambda i, k: (0, 0))],
        out_specs=pl.BlockSpec((tm, D_MODEL), lambda i, k: (i, 0)),
        out_shape=jax.ShapeDtypeStruct((n_tok, D_MODEL), F32),
        scratch_shapes=[pltpu.VMEM((tm, D_MODEL), BF16),
                        pltpu.VMEM((tm, D_MODEL), F32),
                        pltpu.VMEM((nk, 8, tc), F32),
                        pltpu.VMEM((nk, 8, tc), F32)],
        compiler_params=_params(2),
        name="ffn",
    )(x1, g_pre, w_up, w_up, conv_w, conv_w, conv_b, conv_b, w_down, g_post)


def _swap_halves(w):
    half = w.shape[-1] // 2
    return jnp.concatenate([w[..., half:], w[..., :half]], axis=-1)


def _prep_weights(w_in, w_uq, w_ukv):
    hd = FOX_HEADS * FOX_HEAD_DIM
    c_q, c_kv, c_pe = Q_LORA, Q_LORA + KV_LORA, Q_LORA + KV_LORA + ROPE_DIM
    c_fk, c_fv, c_fl = c_pe + hd, c_pe + 2 * hd, c_pe + 3 * hd
    c_g = c_fl + FOX_HEADS
    w_pe = w_in[:, c_kv:c_pe]
    pad = jnp.zeros((D_MODEL, SMALL_COLS - (c_pe + ROPE_DIM + FOX_HEADS)), F32)
    w_small = jnp.concatenate(
        [w_in[:, :c_kv], w_pe, _swap_halves(w_pe), w_in[:, c_fl:c_g], pad], axis=1).astype(BF16)
    w_main = jnp.concatenate(
        [w_in[:, c_g:], w_in[:, c_pe:c_fk] * FOX_HEAD_DIM ** -0.5, w_in[:, c_fk:c_fl]],
        axis=1).astype(BF16)

    uq = w_uq.reshape(Q_LORA, MLA_HEADS, NOPE_DIM + ROPE_DIM) * (NOPE_DIM + ROPE_DIM) ** -0.5
    uq = jnp.concatenate([uq, _swap_halves(uq[..., NOPE_DIM:])], axis=-1)
    w_uq2 = uq.reshape(Q_LORA, MLA_HEADS * MLA_QK_PAD).astype(BF16)
    ukv = w_ukv.reshape(KV_LORA, MLA_HEADS, NOPE_DIM + V_DIM)
    w_ukv2 = jnp.concatenate([ukv[..., :NOPE_DIM].reshape(KV_LORA, -1),
                              ukv[..., NOPE_DIM:].reshape(KV_LORA, -1)], axis=1).astype(BF16)
    return w_small, w_main, w_uq2, w_ukv2


def kernel(x, positions, pre_mix_norm, w_in, q_a_norm, w_uq, kv_a_norm, w_ukv, b_forget, b_gate,
           w_branch_mla, w_branch_fox, w_out, post_mix_norm, pre_ffn_norm, w_up, conv_w, conv_b,
           w_down, post_ffn_norm):
    batch, seq, _ = x.shape
    n_tok = batch * seq
    tm = min(512, seq)
    attn_blk = min(512, seq)

    w_small, w_main, w_uq2, w_ukv2 = _prep_weights(w_in, w_uq, w_ukv)
    x2 = x.reshape(n_tok, D_MODEL)
    pos2 = positions.reshape(n_tok, 1)
    row = lambda v: v.reshape(1, -1).astype(F32)
    inv_freq = 1.0 / (ROPE_THETA ** (jnp.arange(0, ROPE_DIM, 2, dtype=F32) / ROPE_DIM))
    invf = jnp.tile(inv_freq, LANES // inv_freq.shape[0]).reshape(1, LANES)
    bf_pad = jnp.zeros((1, LANES), F32).at[0, :FOX_HEADS].set(b_forget.astype(F32))

    proj = _proj_main(x2, row(pre_mix_norm), w_main, tm, 1024)
    q_mla, k_mla, v_mla, c = _mla_prep(x2, pos2, row(pre_mix_norm), w_small, row(q_a_norm), w_uq2,
                                       row(kv_a_norm), w_ukv2, bf_pad, invf, batch, seq, tm)

    o_mla = _attention(q_mla.reshape(batch, seq, -1), k_mla.reshape(batch, seq, -1),
                       v_mla.reshape(batch, seq, -1), None, dk=MLA_QK_PAD, q_blk0=0, k_blk0=0,
                       v_blk0=0, heads=MLA_HEADS, unit=CHUNK, blk=attn_blk, name="attn_mla")
    proj3 = proj.reshape(batch, seq, MAIN_COLS)
    fq0 = 2 * D_MODEL // FOX_HEAD_DIM
    o_fox = _attention(proj3, proj3, proj3, c.reshape(batch, FOX_HEADS, 1, seq), dk=FOX_HEAD_DIM,
                       q_blk0=fq0, k_blk0=fq0 + FOX_HEADS, v_blk0=fq0 + 2 * FOX_HEADS,
                       heads=FOX_HEADS, unit=1, blk=attn_blk, name="attn_fox")

    x1 = _merge(x2, o_mla.reshape(n_tok, -1), o_fox.reshape(n_tok, -1), proj, row(b_gate),
                w_branch_mla.astype(BF16), w_branch_fox.astype(BF16), w_out.astype(BF16),
                row(post_mix_norm), tm)
    out = _ffn(x1, row(pre_ffn_norm), w_up.astype(BF16), conv_w.astype(F32), row(conv_b),
               w_down.astype(BF16), row(post_ffn_norm), seq, tm, 512)
    return out.reshape(batch, seq, D_MODEL)
```

```python
import collections
import functools
import math

import jax
import jax.numpy as jnp
import numpy as np
from jax import lax
from jax.experimental import pallas as pl
from jax.experimental.pallas import tpu as pltpu

D_MODEL = 2048
CHUNK = 64
MLA_HEADS = 8
Q_LORA = 512
KV_LORA = 256
NOPE_DIM = 128
ROPE_DIM = 64
V_DIM = 128
ROPE_THETA = 10000.0
FOX_HEADS = 8
FOX_HEAD_DIM = 128
D_FF = 5632
CONV_WIDTH = 3
EPS = 1e-6
NEG_INF = -1e30
LOG2E = math.log2(math.e)

LANES = 128
SUBLANES = 8
QK_WIDTH = 256
SMALL_COLS = 1024
MAIN_COLS = 2 * D_MODEL + 3 * FOX_HEADS * FOX_HEAD_DIM
DECAY_PIECES = 3
VMEM_LIMIT = 56 * 1024 * 1024

F32 = jnp.float32
BF16 = jnp.bfloat16

Tiles = collections.namedtuple("Tiles", "proj_rows proj_cols prep_rows attn_blk merge_rows ffn_rows ffn_sub ffn_cols")


def _tiles(seq):
    return Tiles(proj_rows=min(1024, seq), proj_cols=1024, prep_rows=min(512, seq),
                 attn_blk=min(512, seq), merge_rows=min(512, seq), ffn_rows=min(512, seq),
                 ffn_sub=min(256, seq), ffn_cols=512)


def _rms(x, g):
    return x * lax.rsqrt(jnp.mean(x * x, axis=-1, keepdims=True) + EPS) * g


def _dot(a, b):
    return jnp.dot(a, b, preferred_element_type=F32)


def _params(n_axes):
    return pltpu.CompilerParams(dimension_semantics=("arbitrary",) * n_axes,
                                vmem_limit_bytes=VMEM_LIMIT)


def _resident(shape):
    return pl.BlockSpec(shape, lambda *_: (0,) * len(shape), pipeline_mode=pl.Buffered(1))


def _proj_main_kernel(x_ref, g_ref, w_ref, o_ref, h_sc):
    @pl.when(pl.program_id(1) == 0)
    def _():
        h_sc[...] = _rms(x_ref[...], g_ref[...]).astype(BF16)

    o_ref[...] = _dot(h_sc[...], w_ref[...]).astype(o_ref.dtype)


def _proj_main(x2, g_pre, w_main, tm, tn):
    n_tok = x2.shape[0]
    return pl.pallas_call(
        _proj_main_kernel,
        grid=(n_tok // tm, MAIN_COLS // tn),
        in_specs=[pl.BlockSpec((tm, D_MODEL), lambda i, j: (i, 0)),
                  pl.BlockSpec((1, D_MODEL), lambda i, j: (0, 0)),
                  pl.BlockSpec((D_MODEL, tn), lambda i, j: (0, j))],
        out_specs=pl.BlockSpec((tm, tn), lambda i, j: (i, j)),
        out_shape=jax.ShapeDtypeStruct((n_tok, MAIN_COLS), BF16),
        scratch_shapes=[pltpu.VMEM((tm, D_MODEL), BF16)],
        compiler_params=_params(2),
        name="proj_main",
    )(x2, g_pre, w_main)


def _cumsum_rows(v):
    n = v.shape[0]
    rows = lax.broadcasted_iota(jnp.int32, v.shape, 0)
    shift = 1
    while shift < n:
        v = v + jnp.where(rows >= shift, pltpu.roll(v, shift, axis=0), 0.0)
        shift *= 2
    return v


def _mla_prep_kernel(x_ref, pos_ref, g_ref, ws_ref, gq_ref, wuq_ref, gkv_ref, wukv_ref, bf_ref,
                     invf_ref, sel_ref, q_ref, k_ref, v_ref, kx_ref, carry_sc, *, blocks_per_seq):
    tm = x_ref.shape[0]
    h = _rms(x_ref[...], g_ref[...]).astype(BF16)
    small = _dot(h, ws_ref[...])

    lane = lax.broadcasted_iota(jnp.int32, (tm, LANES), 1)
    ang = pos_ref[...].astype(F32) * invf_ref[...]
    sin_part = jnp.where(lane < 96, -jnp.sin(ang), jnp.sin(ang))
    rot = jnp.where(lane < 64, jnp.cos(ang), sin_part)

    def rope(tile):
        p = tile * rot
        return p + pltpu.roll(p, 64, axis=1)

    qn = _rms(small[:, :Q_LORA], gq_ref[...]).astype(BF16)
    q_all = _dot(qn, wuq_ref[...])
    for hd in range(MLA_HEADS):
        c0 = hd * QK_WIDTH
        q_ref[:, c0:c0 + LANES] = q_all[:, c0:c0 + LANES].astype(BF16)
        q_ref[:, c0 + LANES:c0 + 2 * LANES] = rope(q_all[:, c0 + LANES:c0 + 2 * LANES]).astype(BF16)

    kvn = _rms(small[:, Q_LORA:Q_LORA + KV_LORA], gkv_ref[...]).astype(BF16)
    kv = _dot(kvn, wukv_ref[...])
    kpe_col = Q_LORA + KV_LORA
    kpe = jnp.where(lane < ROPE_DIM, rope(small[:, kpe_col:kpe_col + LANES]), 0.0).astype(BF16)
    for hd in range(MLA_HEADS):
        c0 = hd * QK_WIDTH
        k_ref[:, c0:c0 + LANES] = kv[:, hd * NOPE_DIM:(hd + 1) * NOPE_DIM].astype(BF16)
        k_ref[:, c0 + LANES:c0 + 2 * LANES] = kpe
    v_ref[...] = kv[:, MLA_HEADS * NOPE_DIM:].astype(BF16)

    @pl.when(pl.program_id(0) % blocks_per_seq == 0)
    def _():
        carry_sc[...] = jnp.zeros_like(carry_sc)

    z = small[:, kpe_col + LANES:kpe_col + 2 * LANES] + bf_ref[...]
    log_f = jnp.minimum(z, 0.0) - jnp.log1p(jnp.exp(-jnp.abs(z)))
    c_blk = _cumsum_rows(log_f) + carry_sc[0:1, :]
    carry_sc[...] = jnp.broadcast_to(c_blk[tm - 1:tm, :], carry_sc.shape)
    c2 = c_blk * LOG2E

    pieces, rest = [], c2
    for _ in range(DECAY_PIECES):
        part = rest.astype(BF16)
        pieces.append(part)
        rest = rest - part.astype(F32)
    kx_ref[...] = _dot(jnp.concatenate(pieces, axis=-1), sel_ref[...]).astype(BF16)


def _mla_prep(x2, pos2, g_pre, w_small, gq, w_uq, gkv, w_ukv, b_forget_pad, invf, seq, tm):
    n_tok = x2.shape[0]
    row = lambda i: (i, 0)
    sel = np.zeros((DECAY_PIECES * LANES, FOX_HEADS * LANES), np.float32)
    for piece in range(DECAY_PIECES):
        for hd in range(FOX_HEADS):
            sel[piece * LANES + hd, hd * LANES + piece] = -1.0
    return pl.pallas_call(
        functools.partial(_mla_prep_kernel, blocks_per_seq=seq // tm),
        grid=(n_tok // tm,),
        in_specs=[pl.BlockSpec((tm, D_MODEL), row),
                  pl.BlockSpec((tm, 1), row),
                  _resident((1, D_MODEL)),
                  _resident((D_MODEL, SMALL_COLS)),
                  _resident((1, Q_LORA)),
                  _resident((Q_LORA, MLA_HEADS * QK_WIDTH)),
                  _resident((1, KV_LORA)),
                  _resident((KV_LORA, MLA_HEADS * (NOPE_DIM + V_DIM))),
                  _resident((1, LANES)),
                  _resident((1, LANES)),
                  _resident((DECAY_PIECES * LANES, FOX_HEADS * LANES))],
        out_specs=[pl.BlockSpec((tm, MLA_HEADS * QK_WIDTH), row),
                   pl.BlockSpec((tm, MLA_HEADS * QK_WIDTH), row),
                   pl.BlockSpec((tm, MLA_HEADS * V_DIM), row),
                   pl.BlockSpec((tm, FOX_HEADS * LANES), row)],
        out_shape=[jax.ShapeDtypeStruct((n_tok, MLA_HEADS * QK_WIDTH), BF16),
                   jax.ShapeDtypeStruct((n_tok, MLA_HEADS * QK_WIDTH), BF16),
                   jax.ShapeDtypeStruct((n_tok, MLA_HEADS * V_DIM), BF16),
                   jax.ShapeDtypeStruct((n_tok, FOX_HEADS * LANES), BF16)],
        scratch_shapes=[pltpu.VMEM((SUBLANES, LANES), F32)],
        compiler_params=_params(1),
        name="mla_prep",
    )(x2, pos2, g_pre, w_small, gq, w_uq, gkv, w_ukv, b_forget_pad, invf, jnp.asarray(sel, BF16))


def _attn_kernel(*refs, blk, unit, use_decay):
    if use_decay:
        q_ref, k_ref, kx_ref, v_ref, o_ref = refs
    else:
        q_ref, k_ref, v_ref, o_ref = refs
        kx_ref = None
    seq = q_ref.shape[1]
    dv = v_ref.shape[2]
    half = blk // 2

    def visible(n_rows, n_cols, row0):
        row = lax.broadcasted_iota(jnp.int32, (n_rows, n_cols), 0) + row0
        col = lax.broadcasted_iota(jnp.int32, (n_rows, n_cols), 1)
        return (col // unit) <= (row // unit)

    mask_a = visible(half, half, 0)
    mask_b = visible(half, blk, half)
    lane = lax.broadcasted_iota(jnp.int32, (half, LANES), 1)
    ones_tile = jnp.where(lane < DECAY_PIECES, 1.0, 0.0).astype(BF16)

    items = []
    for qi in range(seq // blk):
        for ki in range(qi):
            items += [(qi * blk, ki * blk, blk, None), (qi * blk + half, ki * blk, blk, None)]
        items += [(qi * blk, qi * blk, half, mask_a), (qi * blk + half, qi * blk, blk, mask_b)]

    queries = {}

    def scores(item):
        q0, k0, size, _ = item
        if q0 not in queries:
            q = q_ref[0, q0:q0 + half, :]
            queries[q0] = jnp.concatenate([q, ones_tile], axis=-1) if use_decay else q
        k = k_ref[0, k0:k0 + size, :]
        if use_decay:
            k = jnp.concatenate([k, kx_ref[0, k0:k0 + size, :]], axis=-1)
        return lax.dot_general(queries[q0], k, (((1,), (1,)), ((), ())),
                               preferred_element_type=F32)

    def update(s, item, carry):
        _, k0, size, mask = item
        m, l, acc = carry
        if mask is not None:
            s = jnp.where(mask, s, NEG_INF)
        m_new = jnp.maximum(m, jnp.max(s, axis=-1, keepdims=True))
        alpha = jnp.exp2(m - m_new)
        p = jnp.exp2(s - m_new)
        l = alpha * l + jnp.sum(p, axis=-1, keepdims=True)
        acc = alpha * acc + _dot(p.astype(BF16), v_ref[0, k0:k0 + size, :])
        return m_new, l, acc

    ahead = 2
    pending = [scores(item) for item in items[:ahead]]
    carries = {}
    for t, item in enumerate(items):
        if t + ahead < len(items):
            pending.append(scores(items[t + ahead]))
        q0 = item[0]
        carry = carries.get(q0) or (jnp.full((half, 1), NEG_INF, F32), jnp.zeros((half, 1), F32),
                                    jnp.zeros((half, dv), F32))
        carries[q0] = update(pending.pop(0), item, carry)
        if item[3] is not None:
            _, l, acc = carries.pop(q0)
            o_ref[0, q0:q0 + half, :] = (acc / l).astype(o_ref.dtype)


def _attention(q_arr, k_arr, kx_arr, v_arr, *, dq, dk, q_blk0, k_blk0, v_blk0, heads, unit, blk, name):
    batch, seq, _ = q_arr.shape
    in_specs = [pl.BlockSpec((1, seq, dq), lambda b, h: (b, 0, q_blk0 + h)),
                pl.BlockSpec((1, seq, dk), lambda b, h: (b, 0, k_blk0 + h))]
    args = [q_arr, k_arr]
    if kx_arr is not None:
        in_specs.append(pl.BlockSpec((1, seq, LANES), lambda b, h: (b, 0, h)))
        args.append(kx_arr)
    in_specs.append(pl.BlockSpec((1, seq, V_DIM), lambda b, h: (b, 0, v_blk0 + h)))
    args.append(v_arr)
    return pl.pallas_call(
        functools.partial(_attn_kernel, blk=blk, unit=unit, use_decay=kx_arr is not None),
        grid=(batch, heads),
        in_specs=in_specs,
        out_specs=pl.BlockSpec((1, seq, V_DIM), lambda b, h: (b, 0, h)),
        out_shape=jax.ShapeDtypeStruct((batch, seq, heads * V_DIM), BF16),
        compiler_params=_params(2),
        name=name,
    )(*args)


def _merge_kernel(x_ref, om_ref, of_ref, gm_ref, gf_ref, bg_ref, wm_ref, wf_ref, wo_ref, gn_ref,
                  o_ref):
    g_mla = jax.nn.sigmoid(gm_ref[...].astype(F32) + bg_ref[:, :D_MODEL])
    g_fox = jax.nn.sigmoid(gf_ref[...].astype(F32) + bg_ref[:, D_MODEL:])
    merged = g_mla * _dot(om_ref[...], wm_ref[...]) + g_fox * _dot(of_ref[...], wf_ref[...])
    y = _dot(merged.astype(BF16), wo_ref[...])
    o_ref[...] = x_ref[...] + _rms(y, gn_ref[...])


def _merge(x2, o_mla, o_fox, proj, b_gate, w_bm, w_bf, w_out, g_post, tm):
    n_tok = x2.shape[0]
    hv = MLA_HEADS * V_DIM
    row = lambda i: (i, 0)
    return pl.pallas_call(
        _merge_kernel,
        grid=(n_tok // tm,),
        in_specs=[pl.BlockSpec((tm, D_MODEL), row),
                  pl.BlockSpec((tm, hv), row),
                  pl.BlockSpec((tm, hv), row),
                  pl.BlockSpec((tm, D_MODEL), lambda i: (i, 0)),
                  pl.BlockSpec((tm, D_MODEL), lambda i: (i, 1)),
                  _resident((1, 2 * D_MODEL)),
                  _resident((hv, D_MODEL)),
                  _resident((hv, D_MODEL)),
                  _resident((D_MODEL, D_MODEL)),
                  _resident((1, D_MODEL))],
        out_specs=pl.BlockSpec((tm, D_MODEL), row),
        out_shape=jax.ShapeDtypeStruct((n_tok, D_MODEL), F32),
        compiler_params=_params(1),
        name="merge",
    )(x2, o_mla, o_fox, proj, proj, b_gate, w_bm, w_bf, w_out, g_post)


def _ffn_kernel(x_ref, gpre_ref, wg_ref, wv_ref, cwg_ref, cwv_ref, cbg_ref, cbv_ref, wd_ref,
                gpost_ref, o_ref, h_sc, halo_g, halo_v, *, blocks_per_seq, sub):
    i = pl.program_id(0)
    k = pl.program_id(1)
    tm = x_ref.shape[0]

    @pl.when(k == 0)
    def _():
        h_sc[...] = _rms(x_ref[...], gpre_ref[...]).astype(BF16)
        o_ref[...] = jnp.zeros_like(o_ref)

    @pl.when(i % blocks_per_seq == 0)
    def _():
        halo_g[k] = jnp.zeros(halo_g.shape[1:], F32)
        halo_v[k] = jnp.zeros(halo_v.shape[1:], F32)

    rows = lax.broadcasted_iota(jnp.int32, (sub, 1), 0)

    def conv(u, prev, cw_ref, cb_ref):
        p1, p2 = prev[SUBLANES - 1:SUBLANES, :], prev[SUBLANES - 2:SUBLANES - 1, :]
        um1 = jnp.where(rows == 0, p1, pltpu.roll(u, 1, axis=0))
        um2 = jnp.where(rows == 0, p2, jnp.where(rows == 1, p1, pltpu.roll(u, 2, axis=0)))
        return cw_ref[0:1, :] * um2 + cw_ref[1:2, :] * um1 + cw_ref[2:3, :] * u + cb_ref[...]

    def up(r):
        h = h_sc[r * sub:(r + 1) * sub, :]
        return _dot(h, wg_ref[...]), _dot(h, wv_ref[...])

    n_sub = tm // sub
    prev_g, prev_v = halo_g[k], halo_v[k]
    ug, uv = up(0)
    for r in range(n_sub):
        nxt = up(r + 1) if r + 1 < n_sub else None
        gate, val = conv(ug, prev_g, cwg_ref, cbg_ref), conv(uv, prev_v, cwv_ref, cbv_ref)
        prev_g, prev_v = ug[sub - SUBLANES:, :], uv[sub - SUBLANES:, :]
        act = (jax.nn.gelu(gate, approximate=True) * val).astype(BF16)
        o_ref[r * sub:(r + 1) * sub, :] += _dot(act, wd_ref[...])
        if nxt is not None:
            ug, uv = nxt
    halo_g[k], halo_v[k] = prev_g, prev_v

    @pl.when(k == pl.num_programs(1) - 1)
    def _():
        o_ref[...] = x_ref[...] + _rms(o_ref[...], gpost_ref[...])


def _ffn(x1, g_pre, w_up, conv_w, conv_b, w_down, g_post, seq, tm, sub, tc):
    n_tok = x1.shape[0]
    nk = D_FF // tc
    return pl.pallas_call(
        functools.partial(_ffn_kernel, blocks_per_seq=seq // tm, sub=sub),
        grid=(n_tok // tm, nk),
        in_specs=[pl.BlockSpec((tm, D_MODEL), lambda i, k: (i, 0)),
                  pl.BlockSpec((1, D_MODEL), lambda i, k: (0, 0)),
                  pl.BlockSpec((D_MODEL, tc), lambda i, k: (0, k)),
                  pl.BlockSpec((D_MODEL, tc), lambda i, k: (0, nk + k)),
                  pl.BlockSpec((CONV_WIDTH, tc), lambda i, k: (0, k)),
                  pl.BlockSpec((CONV_WIDTH, tc), lambda i, k: (0, nk + k)),
                  pl.BlockSpec((1, tc), lambda i, k: (0, k)),
                  pl.BlockSpec((1, tc), lambda i, k: (0, nk + k)),
                  pl.BlockSpec((tc, D_MODEL), lambda i, k: (k, 0)),
                  pl.BlockSpec((1, D_MODEL), lambda i, k: (0, 0))],
        out_specs=pl.BlockSpec((tm, D_MODEL), lambda i, k: (i, 0)),
        out_shape=jax.ShapeDtypeStruct((n_tok, D_MODEL), F32),
        scratch_shapes=[pltpu.VMEM((tm, D_MODEL), BF16),
                        pltpu.VMEM((nk, SUBLANES, tc), F32),
                        pltpu.VMEM((nk, SUBLANES, tc), F32)],
        compiler_params=_params(2),
        name="ffn",
    )(x1, g_pre, w_up, w_up, conv_w, conv_w, conv_b, conv_b, w_down, g_post)


def _swap_halves(w):
    half = w.shape[-1] // 2
    return jnp.concatenate([w[..., half:], w[..., :half]], axis=-1)


def _prep_weights(w_in, w_uq, w_ukv):
    hd = FOX_HEADS * FOX_HEAD_DIM
    c_q, c_kv, c_pe = Q_LORA, Q_LORA + KV_LORA, Q_LORA + KV_LORA + ROPE_DIM
    c_fk, c_fv, c_fl = c_pe + hd, c_pe + 2 * hd, c_pe + 3 * hd
    c_g = c_fl + FOX_HEADS
    w_pe = w_in[:, c_kv:c_pe]
    pad = jnp.zeros((D_MODEL, SMALL_COLS - (c_pe + ROPE_DIM + FOX_HEADS)), F32)
    w_small = jnp.concatenate(
        [w_in[:, :c_kv], w_pe, _swap_halves(w_pe), w_in[:, c_fl:c_g], pad], axis=1).astype(BF16)
    w_main = jnp.concatenate(
        [w_in[:, c_g:], w_in[:, c_pe:c_fk] * (LOG2E * FOX_HEAD_DIM ** -0.5), w_in[:, c_fk:c_fl]],
        axis=1).astype(BF16)

    uq = w_uq.reshape(Q_LORA, MLA_HEADS, NOPE_DIM + ROPE_DIM) * (LOG2E * (NOPE_DIM + ROPE_DIM) ** -0.5)
    uq = jnp.concatenate([uq, _swap_halves(uq[..., NOPE_DIM:])], axis=-1)
    w_uq2 = uq.reshape(Q_LORA, MLA_HEADS * QK_WIDTH).astype(BF16)
    ukv = w_ukv.reshape(KV_LORA, MLA_HEADS, NOPE_DIM + V_DIM)
    w_ukv2 = jnp.concatenate([ukv[..., :NOPE_DIM].reshape(KV_LORA, -1),
                              ukv[..., NOPE_DIM:].reshape(KV_LORA, -1)], axis=1).astype(BF16)
    return w_small, w_main, w_uq2, w_ukv2


def kernel(x, positions, pre_mix_norm, w_in, q_a_norm, w_uq, kv_a_norm, w_ukv, b_forget, b_gate,
           w_branch_mla, w_branch_fox, w_out, post_mix_norm, pre_ffn_norm, w_up, conv_w, conv_b,
           w_down, post_ffn_norm):
    batch, seq, _ = x.shape
    n_tok = batch * seq
    t = _tiles(seq)

    w_small, w_main, w_uq2, w_ukv2 = _prep_weights(w_in, w_uq, w_ukv)
    x2 = x.reshape(n_tok, D_MODEL)
    pos2 = positions.reshape(n_tok, 1)
    row = lambda v: v.reshape(1, -1).astype(F32)
    inv_freq = 1.0 / (ROPE_THETA ** (jnp.arange(0, ROPE_DIM, 2, dtype=F32) / ROPE_DIM))
    invf = jnp.tile(inv_freq, LANES // inv_freq.shape[0]).reshape(1, LANES)
    bf_pad = jnp.zeros((1, LANES), F32).at[0, :FOX_HEADS].set(b_forget.astype(F32))

    proj = _proj_main(x2, row(pre_mix_norm), w_main, t.proj_rows, t.proj_cols)
    q_mla, k_mla, v_mla, kx = _mla_prep(x2, pos2, row(pre_mix_norm), w_small, row(q_a_norm), w_uq2,
                                        row(kv_a_norm), w_ukv2, bf_pad, invf, seq, t.prep_rows)

    per_seq = lambda a: a.reshape(batch, seq, -1)
    o_mla = _attention(per_seq(q_mla), per_seq(k_mla), None, per_seq(v_mla), dq=QK_WIDTH,
                       dk=QK_WIDTH, q_blk0=0, k_blk0=0, v_blk0=0, heads=MLA_HEADS, unit=CHUNK,
                       blk=t.attn_blk, name="attn_mla")
    proj3 = per_seq(proj)
    fq0 = 2 * D_MODEL // FOX_HEAD_DIM
    o_fox = _attention(proj3, proj3, per_seq(kx), proj3, dq=FOX_HEAD_DIM, dk=FOX_HEAD_DIM,
                       q_blk0=fq0, k_blk0=fq0 + FOX_HEADS, v_blk0=fq0 + 2 * FOX_HEADS,
                       heads=FOX_HEADS, unit=1, blk=t.attn_blk, name="attn_fox")

    x1 = _merge(x2, o_mla.reshape(n_tok, -1), o_fox.reshape(n_tok, -1), proj, row(b_gate),
                w_branch_mla.astype(BF16), w_branch_fox.astype(BF16), w_out.astype(BF16),
                row(post_mix_norm), t.merge_rows)
    out = _ffn(x1, row(pre_ffn_norm), w_up.astype(BF16), conv_w.astype(F32), row(conv_b),
               w_down.astype(BF16), row(post_ffn_norm), seq, t.ffn_rows, t.ffn_sub, t.ffn_cols)
    return out.reshape(batch, seq, D_MODEL)
```

```python
import collections
import functools
import math

import jax
import jax.numpy as jnp
import numpy as np
from jax import lax
from jax.experimental import pallas as pl
from jax.experimental.pallas import tpu as pltpu

D_MODEL = 2048
CHUNK = 64
MLA_HEADS = 8
Q_LORA = 512
KV_LORA = 256
NOPE_DIM = 128
ROPE_DIM = 64
V_DIM = 128
ROPE_THETA = 10000.0
FOX_HEADS = 8
FOX_HEAD_DIM = 128
D_FF = 5632
CONV_WIDTH = 3
EPS = 1e-6
NEG_INF = -1e30
LOG2E = math.log2(math.e)

LANES = 128
SUBLANES = 8
QK_WIDTH = 256
SMALL_COLS = 1024
MAIN_COLS = 2 * D_MODEL + 3 * FOX_HEADS * FOX_HEAD_DIM
DECAY_PIECES = 3
VMEM_LIMIT = 56 * 1024 * 1024

F32 = jnp.float32
BF16 = jnp.bfloat16

Tiles = collections.namedtuple(
    "Tiles", "proj_rows proj_cols prep_rows attn_blk merge_rows merge_sub ffn_rows ffn_sub ffn_cols")


def _tiles(seq):
    return Tiles(proj_rows=min(1024, seq), proj_cols=1024, prep_rows=min(512, seq),
                 attn_blk=min(512, seq), merge_rows=min(512, seq), merge_sub=min(256, seq),
                 ffn_rows=min(512, seq), ffn_sub=min(256, seq), ffn_cols=512)


def _rms(x, g):
    return x * lax.rsqrt(jnp.mean(x * x, axis=-1, keepdims=True) + EPS) * g


def _dot(a, b):
    return jnp.dot(a, b, preferred_element_type=F32)


def _params(n_axes):
    return pltpu.CompilerParams(dimension_semantics=("arbitrary",) * n_axes,
                                vmem_limit_bytes=VMEM_LIMIT)


def _resident(shape):
    return pl.BlockSpec(shape, lambda *_: (0,) * len(shape), pipeline_mode=pl.Buffered(1))


def _proj_main_kernel(x_ref, g_ref, w_ref, o_ref, h_sc):
    @pl.when(pl.program_id(1) == 0)
    def _():
        h_sc[...] = _rms(x_ref[...], g_ref[...]).astype(BF16)

    o_ref[...] = _dot(h_sc[...], w_ref[...]).astype(o_ref.dtype)


def _proj_main(x2, g_pre, w_main, tm, tn):
    n_tok = x2.shape[0]
    return pl.pallas_call(
        _proj_main_kernel,
        grid=(n_tok // tm, MAIN_COLS // tn),
        in_specs=[pl.BlockSpec((tm, D_MODEL), lambda i, j: (i, 0)),
                  pl.BlockSpec((1, D_MODEL), lambda i, j: (0, 0)),
                  pl.BlockSpec((D_MODEL, tn), lambda i, j: (0, j))],
        out_specs=pl.BlockSpec((tm, tn), lambda i, j: (i, j)),
        out_shape=jax.ShapeDtypeStruct((n_tok, MAIN_COLS), BF16),
        scratch_shapes=[pltpu.VMEM((tm, D_MODEL), BF16)],
        compiler_params=_params(2),
        name="proj_main",
    )(x2, g_pre, w_main)


def _cumsum_rows(v):
    n = v.shape[0]
    rows = lax.broadcasted_iota(jnp.int32, v.shape, 0)
    shift = 1
    while shift < n:
        v = v + jnp.where(rows >= shift, pltpu.roll(v, shift, axis=0), 0.0)
        shift *= 2
    return v


def _mla_prep_kernel(x_ref, pos_ref, g_ref, ws_ref, gq_ref, wuq_ref, gkv_ref, wukv_ref, bf_ref,
                     invf_ref, sel_ref, q_ref, k_ref, v_ref, kx_ref, carry_sc, *, blocks_per_seq):
    tm = x_ref.shape[0]
    h = _rms(x_ref[...], g_ref[...]).astype(BF16)
    small = _dot(h, ws_ref[...])

    lane = lax.broadcasted_iota(jnp.int32, (tm, LANES), 1)
    hl = lax.broadcasted_iota(jnp.int32, (tm // 2, LANES), 1)
    pos_top = jnp.broadcast_to(pos_ref[:tm // 2, :].astype(F32), (tm // 2, LANES))
    pos_bot = jnp.broadcast_to(pos_ref[tm // 2:, :].astype(F32), (tm // 2, LANES))
    ang = jnp.where(hl < 64, pos_top, pos_bot) * invf_ref[...]
    cos, sin = jnp.cos(ang), jnp.sin(ang)
    sin = jnp.where((hl & 32) == 0, -sin, sin)
    cos_sw, sin_sw = pltpu.roll(cos, 64, axis=1), pltpu.roll(sin, 64, axis=1)
    rot = jnp.concatenate([jnp.where(hl < 64, cos, sin_sw), jnp.where(hl < 64, cos_sw, sin)], axis=0)

    def rope(tile):
        p = tile * rot
        return p + pltpu.roll(p, 64, axis=1)

    qn = _rms(small[:, :Q_LORA], gq_ref[...]).astype(BF16)
    q_all = _dot(qn, wuq_ref[...])
    for hd in range(MLA_HEADS):
        c0 = hd * QK_WIDTH
        q_ref[:, c0:c0 + LANES] = q_all[:, c0:c0 + LANES].astype(BF16)
        q_ref[:, c0 + LANES:c0 + 2 * LANES] = rope(q_all[:, c0 + LANES:c0 + 2 * LANES]).astype(BF16)

    kvn = _rms(small[:, Q_LORA:Q_LORA + KV_LORA], gkv_ref[...]).astype(BF16)
    kv = _dot(kvn, wukv_ref[...])
    kpe_col = Q_LORA + KV_LORA
    kpe = jnp.where(lane < ROPE_DIM, rope(small[:, kpe_col:kpe_col + LANES]), 0.0).astype(BF16)
    for hd in range(MLA_HEADS):
        c0 = hd * QK_WIDTH
        k_ref[:, c0:c0 + LANES] = kv[:, hd * NOPE_DIM:(hd + 1) * NOPE_DIM].astype(BF16)
        k_ref[:, c0 + LANES:c0 + 2 * LANES] = kpe
    v_ref[...] = kv[:, MLA_HEADS * NOPE_DIM:].astype(BF16)

    @pl.when(pl.program_id(0) % blocks_per_seq == 0)
    def _():
        carry_sc[...] = jnp.zeros_like(carry_sc)

    z = small[:, kpe_col + LANES:kpe_col + 2 * LANES] + bf_ref[...]
    log_f = jnp.minimum(z, 0.0) - jnp.log1p(jnp.exp(-jnp.abs(z)))
    c_blk = _cumsum_rows(log_f) + carry_sc[0:1, :]
    carry_sc[...] = jnp.broadcast_to(c_blk[tm - 1:tm, :], carry_sc.shape)
    c2 = c_blk * LOG2E

    pieces, rest = [], c2
    for _ in range(DECAY_PIECES):
        part = rest.astype(BF16)
        pieces.append(part)
        rest = rest - part.astype(F32)
    kx_ref[...] = _dot(jnp.concatenate(pieces, axis=-1), sel_ref[...]).astype(BF16)


def _mla_prep(x2, pos2, g_pre, w_small, gq, w_uq, gkv, w_ukv, b_forget_pad, invf, seq, tm):
    n_tok = x2.shape[0]
    row = lambda i: (i, 0)
    sel = np.zeros((DECAY_PIECES * LANES, FOX_HEADS * LANES), np.float32)
    for piece in range(DECAY_PIECES):
        for hd in range(FOX_HEADS):
            sel[piece * LANES + hd, hd * LANES + piece] = -1.0
    return pl.pallas_call(
        functools.partial(_mla_prep_kernel, blocks_per_seq=seq // tm),
        grid=(n_tok // tm,),
        in_specs=[pl.BlockSpec((tm, D_MODEL), row),
                  pl.BlockSpec((tm, 1), row),
                  _resident((1, D_MODEL)),
                  _resident((D_MODEL, SMALL_COLS)),
                  _resident((1, Q_LORA)),
                  _resident((Q_LORA, MLA_HEADS * QK_WIDTH)),
                  _resident((1, KV_LORA)),
                  _resident((KV_LORA, MLA_HEADS * (NOPE_DIM + V_DIM))),
                  _resident((1, LANES)),
                  _resident((1, LANES)),
                  _resident((DECAY_PIECES * LANES, FOX_HEADS * LANES))],
        out_specs=[pl.BlockSpec((tm, MLA_HEADS * QK_WIDTH), row),
                   pl.BlockSpec((tm, MLA_HEADS * QK_WIDTH), row),
                   pl.BlockSpec((tm, MLA_HEADS * V_DIM), row),
                   pl.BlockSpec((tm, FOX_HEADS * LANES), row)],
        out_shape=[jax.ShapeDtypeStruct((n_tok, MLA_HEADS * QK_WIDTH), BF16),
                   jax.ShapeDtypeStruct((n_tok, MLA_HEADS * QK_WIDTH), BF16),
                   jax.ShapeDtypeStruct((n_tok, MLA_HEADS * V_DIM), BF16),
                   jax.ShapeDtypeStruct((n_tok, FOX_HEADS * LANES), BF16)],
        scratch_shapes=[pltpu.VMEM((SUBLANES, LANES), F32)],
        compiler_params=_params(1),
        name="mla_prep",
    )(x2, pos2, g_pre, w_small, gq, w_uq, gkv, w_ukv, b_forget_pad, invf, jnp.asarray(sel, BF16))


def _attn_kernel(*refs, blk, unit, use_decay):
    if use_decay:
        q_ref, k_ref, kx_ref, v_ref, o_ref = refs
    else:
        q_ref, k_ref, v_ref, o_ref = refs
        kx_ref = None
    seq = q_ref.shape[1]
    dv = v_ref.shape[2]
    half = blk // 2

    def visible(n_rows, n_cols, row0):
        row = lax.broadcasted_iota(jnp.int32, (n_rows, n_cols), 0) + row0
        col = lax.broadcasted_iota(jnp.int32, (n_rows, n_cols), 1)
        return (col // unit) <= (row // unit)

    mask_a = visible(half, half, 0)
    mask_b = visible(half, blk, half)
    lane = lax.broadcasted_iota(jnp.int32, (half, LANES), 1)
    ones_tile = jnp.where(lane < DECAY_PIECES, 1.0, 0.0).astype(BF16)

    items = []
    for qi in range(seq // blk):
        for ki in range(qi):
            items += [(qi * blk, ki * blk, blk, None), (qi * blk + half, ki * blk, blk, None)]
        items += [(qi * blk, qi * blk, half, mask_a), (qi * blk + half, qi * blk, blk, mask_b)]

    queries = {}

    def scores(item):
        q0, k0, size, _ = item
        if q0 not in queries:
            q = q_ref[0, q0:q0 + half, :]
            queries[q0] = jnp.concatenate([q, ones_tile], axis=-1) if use_decay else q
        k = k_ref[0, k0:k0 + size, :]
        if use_decay:
            k = jnp.concatenate([k, kx_ref[0, k0:k0 + size, :]], axis=-1)
        return lax.dot_general(queries[q0], k, (((1,), (1,)), ((), ())),
                               preferred_element_type=F32)

    def update(s, item, carry):
        _, k0, size, mask = item
        m, l, acc = carry
        if mask is not None:
            s = jnp.where(mask, s, NEG_INF)
        m_new = jnp.maximum(m, jnp.max(s, axis=-1, keepdims=True))
        alpha = jnp.exp2(m - m_new)
        p = jnp.exp2(s - m_new)
        l = alpha * l + jnp.sum(p, axis=-1, keepdims=True)
        acc = alpha * acc + _dot(p.astype(BF16), v_ref[0, k0:k0 + size, :])
        return m_new, l, acc

    ahead = 2
    pending = [scores(item) for item in items[:ahead]]
    carries = {}
    for t, item in enumerate(items):
        if t + ahead < len(items):
            pending.append(scores(items[t + ahead]))
        q0 = item[0]
        carry = carries.get(q0) or (jnp.full((half, 1), NEG_INF, F32), jnp.zeros((half, 1), F32),
                                    jnp.zeros((half, dv), F32))
        carries[q0] = update(pending.pop(0), item, carry)
        if item[3] is not None:
            _, l, acc = carries.pop(q0)
            o_ref[0, q0:q0 + half, :] = (acc / l).astype(o_ref.dtype)


def _attention(q_arr, k_arr, kx_arr, v_arr, *, dq, dk, q_blk0, k_blk0, v_blk0, heads, unit, blk, name):
    batch, seq, _ = q_arr.shape
    in_specs = [pl.BlockSpec((1, seq, dq), lambda b, h: (b, 0, q_blk0 + h)),
                pl.BlockSpec((1, seq, dk), lambda b, h: (b, 0, k_blk0 + h))]
    args = [q_arr, k_arr]
    if kx_arr is not None:
        in_specs.append(pl.BlockSpec((1, seq, LANES), lambda b, h: (b, 0, h)))
        args.append(kx_arr)
    in_specs.append(pl.BlockSpec((1, seq, V_DIM), lambda b, h: (b, 0, v_blk0 + h)))
    args.append(v_arr)
    return pl.pallas_call(
        functools.partial(_attn_kernel, blk=blk, unit=unit, use_decay=kx_arr is not None),
        grid=(batch, heads),
        in_specs=in_specs,
        out_specs=pl.BlockSpec((1, seq, V_DIM), lambda b, h: (b, 0, h)),
        out_shape=jax.ShapeDtypeStruct((batch, seq, heads * V_DIM), BF16),
        compiler_params=_params(2),
        name=name,
    )(*args)


def _merge_kernel(x_ref, om_ref, of_ref, gm_ref, gf_ref, bg_ref, wm_ref, wf_ref, wo_ref, gn_ref,
                  o_ref, *, sub):
    def branches(r):
        rs = slice(r * sub, (r + 1) * sub)
        return _dot(om_ref[rs, :], wm_ref[...]), _dot(of_ref[rs, :], wf_ref[...])

    n_sub = x_ref.shape[0] // sub
    cur = branches(0)
    for r in range(n_sub):
        rs = slice(r * sub, (r + 1) * sub)
        nxt = branches(r + 1) if r + 1 < n_sub else None
        g_mla = jax.nn.sigmoid(gm_ref[rs, :].astype(F32) + bg_ref[:, :D_MODEL])
        g_fox = jax.nn.sigmoid(gf_ref[rs, :].astype(F32) + bg_ref[:, D_MODEL:])
        merged = g_mla * cur[0] + g_fox * cur[1]
        y = _dot(merged.astype(BF16), wo_ref[...])
        o_ref[rs, :] = x_ref[rs, :] + _rms(y, gn_ref[...])
        cur = nxt


def _merge(x2, o_mla, o_fox, proj, b_gate, w_bm, w_bf, w_out, g_post, tm, sub):
    n_tok = x2.shape[0]
    hv = MLA_HEADS * V_DIM
    row = lambda i: (i, 0)
    return pl.pallas_call(
        functools.partial(_merge_kernel, sub=sub),
        grid=(n_tok // tm,),
        in_specs=[pl.BlockSpec((tm, D_MODEL), row),
                  pl.BlockSpec((tm, hv), row),
                  pl.BlockSpec((tm, hv), row),
                  pl.BlockSpec((tm, D_MODEL), lambda i: (i, 0)),
                  pl.BlockSpec((tm, D_MODEL), lambda i: (i, 1)),
                  _resident((1, 2 * D_MODEL)),
                  _resident((hv, D_MODEL)),
                  _resident((hv, D_MODEL)),
                  _resident((D_MODEL, D_MODEL)),
                  _resident((1, D_MODEL))],
        out_specs=pl.BlockSpec((tm, D_MODEL), row),
        out_shape=jax.ShapeDtypeStruct((n_tok, D_MODEL), F32),
        compiler_params=_params(1),
        name="merge",
    )(x2, o_mla, o_fox, proj, proj, b_gate, w_bm, w_bf, w_out, g_post)


def _ffn_kernel(x_ref, gpre_ref, wu_ref, cw_ref, cb_ref, wd_ref, gpost_ref, o_ref, h_sc, halo, u_sc,
                *, blocks_per_seq, sub):
    i = pl.program_id(0)
    k = pl.program_id(1)
    tm = x_ref.shape[0]
    tc = wd_ref.shape[0]

    @pl.when(k == 0)
    def _():
        h_sc[...] = _rms(x_ref[...], gpre_ref[...]).astype(BF16)
        o_ref[...] = jnp.zeros_like(o_ref)

    @pl.when(i % blocks_per_seq == 0)
    def _():
        halo[k] = jnp.zeros(halo.shape[1:], F32)

    cw, cb = cw_ref[0], cb_ref[0]
    u_sc[0:SUBLANES, :] = halo[k]

    def up(r):
        u_sc[SUBLANES + r * sub:SUBLANES + (r + 1) * sub, :] = _dot(h_sc[r * sub:(r + 1) * sub, :],
                                                                   wu_ref[0])

    def conv(r):
        taps = [u_sc[SUBLANES + r * sub - back:SUBLANES + (r + 1) * sub - back, :]
                for back in (2, 1, 0)]
        return cw[0:1, :] * taps[0] + cw[1:2, :] * taps[1] + cw[2:3, :] * taps[2] + cb

    n_sub = tm // sub
    up(0)
    for r in range(n_sub):
        if r + 1 < n_sub:
            up(r + 1)
        c = conv(r)
        act = (jax.nn.gelu(c[:, :tc], approximate=True) * c[:, tc:]).astype(BF16)
        o_ref[r * sub:(r + 1) * sub, :] += _dot(act, wd_ref[...])
    halo[k] = u_sc[tm:tm + SUBLANES, :]

    @pl.when(k == pl.num_programs(1) - 1)
    def _():
        o_ref[...] = x_ref[...] + _rms(o_ref[...], gpost_ref[...])


def _ffn(x1, g_pre, w_up, conv_w, conv_b, w_down, g_post, seq, tm, sub, tc):
    n_tok = x1.shape[0]
    nk = D_FF // tc
    chunk = lambda i, k: (k, 0, 0)
    return pl.pallas_call(
        functools.partial(_ffn_kernel, blocks_per_seq=seq // tm, sub=sub),
        grid=(n_tok // tm, nk),
        in_specs=[pl.BlockSpec((tm, D_MODEL), lambda i, k: (i, 0)),
                  pl.BlockSpec((1, D_MODEL), lambda i, k: (0, 0)),
                  pl.BlockSpec((1, D_MODEL, 2 * tc), chunk),
                  pl.BlockSpec((1, CONV_WIDTH, 2 * tc), chunk),
                  pl.BlockSpec((1, 1, 2 * tc), chunk),
                  pl.BlockSpec((tc, D_MODEL), lambda i, k: (k, 0)),
                  pl.BlockSpec((1, D_MODEL), lambda i, k: (0, 0))],
        out_specs=pl.BlockSpec((tm, D_MODEL), lambda i, k: (i, 0)),
        out_shape=jax.ShapeDtypeStruct((n_tok, D_MODEL), F32),
        scratch_shapes=[pltpu.VMEM((tm, D_MODEL), BF16),
                        pltpu.VMEM((nk, SUBLANES, 2 * tc), F32),
                        pltpu.VMEM((SUBLANES + tm, 2 * tc), F32)],
        compiler_params=_params(2),
        name="ffn",
    )(x1, g_pre, w_up, conv_w, conv_b, w_down, g_post)


def _chunk_major(w, tc):
    rows = w.shape[0]
    w = w.reshape(rows, 2, D_FF // tc, tc)
    return jnp.transpose(w, (2, 0, 1, 3)).reshape(D_FF // tc, rows, 2 * tc)


def _swap_halves(w):
    half = w.shape[-1] // 2
    return jnp.concatenate([w[..., half:], w[..., :half]], axis=-1)


def _prep_weights(w_in, w_uq, w_ukv):
    hd = FOX_HEADS * FOX_HEAD_DIM
    c_q, c_kv, c_pe = Q_LORA, Q_LORA + KV_LORA, Q_LORA + KV_LORA + ROPE_DIM
    c_fk, c_fv, c_fl = c_pe + hd, c_pe + 2 * hd, c_pe + 3 * hd
    c_g = c_fl + FOX_HEADS
    w_pe = w_in[:, c_kv:c_pe]
    pad = jnp.zeros((D_MODEL, SMALL_COLS - (c_pe + ROPE_DIM + FOX_HEADS)), F32)
    w_small = jnp.concatenate(
        [w_in[:, :c_kv], w_pe, _swap_halves(w_pe), w_in[:, c_fl:c_g], pad], axis=1).astype(BF16)
    w_main = jnp.concatenate(
        [w_in[:, c_g:], w_in[:, c_pe:c_fk] * (LOG2E * FOX_HEAD_DIM ** -0.5), w_in[:, c_fk:c_fl]],
        axis=1).astype(BF16)

    uq = w_uq.reshape(Q_LORA, MLA_HEADS, NOPE_DIM + ROPE_DIM) * (LOG2E * (NOPE_DIM + ROPE_DIM) ** -0.5)
    uq = jnp.concatenate([uq, _swap_halves(uq[..., NOPE_DIM:])], axis=-1)
    w_uq2 = uq.reshape(Q_LORA, MLA_HEADS * QK_WIDTH).astype(BF16)
    ukv = w_ukv.reshape(KV_LORA, MLA_HEADS, NOPE_DIM + V_DIM)
    w_ukv2 = jnp.concatenate([ukv[..., :NOPE_DIM].reshape(KV_LORA, -1),
                              ukv[..., NOPE_DIM:].reshape(KV_LORA, -1)], axis=1).astype(BF16)
    return w_small, w_main, w_uq2, w_ukv2


def kernel(x, positions, pre_mix_norm, w_in, q_a_norm, w_uq, kv_a_norm, w_ukv, b_forget, b_gate,
           w_branch_mla, w_branch_fox, w_out, post_mix_norm, pre_ffn_norm, w_up, conv_w, conv_b,
           w_down, post_ffn_norm):
    batch, seq, _ = x.shape
    n_tok = batch * seq
    t = _tiles(seq)

    w_small, w_main, w_uq2, w_ukv2 = _prep_weights(w_in, w_uq, w_ukv)
    x2 = x.reshape(n_tok, D_MODEL)
    pos2 = positions.reshape(n_tok, 1)
    row = lambda v: v.reshape(1, -1).astype(F32)
    inv_freq = 1.0 / (ROPE_THETA ** (jnp.arange(0, ROPE_DIM, 2, dtype=F32) / ROPE_DIM))
    invf = jnp.tile(inv_freq, LANES // inv_freq.shape[0]).reshape(1, LANES)
    bf_pad = jnp.zeros((1, LANES), F32).at[0, :FOX_HEADS].set(b_forget.astype(F32))

    proj = _proj_main(x2, row(pre_mix_norm), w_main, t.proj_rows, t.proj_cols)
    q_mla, k_mla, v_mla, kx = _mla_prep(x2, pos2, row(pre_mix_norm), w_small, row(q_a_norm), w_uq2,
                                        row(kv_a_norm), w_ukv2, bf_pad, invf, seq, t.prep_rows)

    per_seq = lambda a: a.reshape(batch, seq, -1)
    o_mla = _attention(per_seq(q_mla), per_seq(k_mla), None, per_seq(v_mla), dq=QK_WIDTH,
                       dk=QK_WIDTH, q_blk0=0, k_blk0=0, v_blk0=0, heads=MLA_HEADS, unit=CHUNK,
                       blk=t.attn_blk, name="attn_mla")
    proj3 = per_seq(proj)
    fq0 = 2 * D_MODEL // FOX_HEAD_DIM
    o_fox = _attention(proj3, proj3, per_seq(kx), proj3, dq=FOX_HEAD_DIM, dk=FOX_HEAD_DIM,
                       q_blk0=fq0, k_blk0=fq0 + FOX_HEADS, v_blk0=fq0 + 2 * FOX_HEADS,
                       heads=FOX_HEADS, unit=1, blk=t.attn_blk, name="attn_fox")

    x1 = _merge(x2, o_mla.reshape(n_tok, -1), o_fox.reshape(n_tok, -1), proj, row(b_gate),
                w_branch_mla.astype(BF16), w_branch_fox.astype(BF16), w_out.astype(BF16),
                row(post_mix_norm), t.merge_rows, t.merge_sub)
    out = _ffn(x1, row(pre_ffn_norm), _chunk_major(w_up.astype(BF16), t.ffn_cols),
               _chunk_major(conv_w.astype(F32), t.ffn_cols), _chunk_major(row(conv_b), t.ffn_cols),
               w_down.astype(BF16), row(post_ffn_norm), seq, t.ffn_rows, t.ffn_sub, t.ffn_cols)
    return out.reshape(batch, seq, D_MODEL)
```

```python
import collections
import functools
import math

import jax
import jax.numpy as jnp
import numpy as np
from jax import lax
from jax.experimental import pallas as pl
from jax.experimental.pallas import tpu as pltpu

D_MODEL = 2048
CHUNK = 64
MLA_HEADS = 8
Q_LORA = 512
KV_LORA = 256
NOPE_DIM = 128
ROPE_DIM = 64
V_DIM = 128
ROPE_THETA = 10000.0
FOX_HEADS = 8
FOX_HEAD_DIM = 128
D_FF = 5632
CONV_WIDTH = 3
EPS = 1e-6
NEG_INF = -1e30
LOG2E = math.log2(math.e)

LANES = 128
SUBLANES = 8
QK_WIDTH = 256
SMALL_COLS = 1024
MAIN_COLS = 2 * D_MODEL + 3 * FOX_HEADS * FOX_HEAD_DIM
DECAY_PIECES = 3
VMEM_LIMIT = 56 * 1024 * 1024

F32 = jnp.float32
BF16 = jnp.bfloat16

Tiles = collections.namedtuple(
    "Tiles", "proj_rows proj_sub proj_cols prep_rows attn_blk merge_rows merge_sub ffn_rows ffn_sub ffn_cols")


def _tiles(seq):
    return Tiles(proj_rows=min(1024, seq), proj_sub=min(256, seq), proj_cols=1024, prep_rows=min(512, seq),
                 attn_blk=min(512, seq), merge_rows=min(512, seq), merge_sub=min(256, seq),
                 ffn_rows=min(512, seq), ffn_sub=min(256, seq), ffn_cols=512)


def _rms(x, g):
    return x * lax.rsqrt(jnp.mean(x * x, axis=-1, keepdims=True) + EPS) * g


def _dot(a, b):
    return jnp.dot(a, b, preferred_element_type=F32)


def _params(n_axes):
    return pltpu.CompilerParams(dimension_semantics=("arbitrary",) * n_axes,
                                vmem_limit_bytes=VMEM_LIMIT)


def _resident(shape):
    return pl.BlockSpec(shape, lambda *_: (0,) * len(shape), pipeline_mode=pl.Buffered(1))


def _proj_main_kernel(x_ref, g_ref, w_ref, o_ref, h_sc, *, sub):
    j = pl.program_id(1)

    @pl.when(j == 0)
    def _():
        for r in range(x_ref.shape[0] // sub):
            rs = slice(r * sub, (r + 1) * sub)
            h_sc[rs, :] = _rms(x_ref[rs, :], g_ref[...]).astype(BF16)
            o_ref[rs, :] = _dot(h_sc[rs, :], w_ref[...]).astype(o_ref.dtype)

    @pl.when(j > 0)
    def _():
        o_ref[...] = _dot(h_sc[...], w_ref[...]).astype(o_ref.dtype)


def _proj_main(x2, g_pre, w_main, tm, sub, tn):
    n_tok = x2.shape[0]
    return pl.pallas_call(
        functools.partial(_proj_main_kernel, sub=sub),
        grid=(n_tok // tm, MAIN_COLS // tn),
        in_specs=[pl.BlockSpec((tm, D_MODEL), lambda i, j: (i, 0)),
                  pl.BlockSpec((1, D_MODEL), lambda i, j: (0, 0)),
                  pl.BlockSpec((D_MODEL, tn), lambda i, j: (0, j))],
        out_specs=pl.BlockSpec((tm, tn), lambda i, j: (i, j)),
        out_shape=jax.ShapeDtypeStruct((n_tok, MAIN_COLS), BF16),
        scratch_shapes=[pltpu.VMEM((tm, D_MODEL), BF16)],
        compiler_params=_params(2),
        name="proj_main",
    )(x2, g_pre, w_main)


def _cumsum_rows(v):
    n = v.shape[0]
    rows = lax.broadcasted_iota(jnp.int32, v.shape, 0)
    shift = 1
    while shift < n:
        v = v + jnp.where(rows >= shift, pltpu.roll(v, shift, axis=0), 0.0)
        shift *= 2
    return v


def _mla_prep_kernel(x_ref, pos_ref, g_ref, ws_ref, gq_ref, wuq_ref, gkv_ref, wukv_ref, bf_ref,
                     invf_ref, sel_ref, q_ref, k_ref, v_ref, kx_ref, carry_sc, *, blocks_per_seq):
    tm = x_ref.shape[0]
    h = _rms(x_ref[...], g_ref[...]).astype(BF16)
    small = _dot(h, ws_ref[...])

    lane = lax.broadcasted_iota(jnp.int32, (tm, LANES), 1)
    hl = lax.broadcasted_iota(jnp.int32, (tm // 2, LANES), 1)
    pos_top = jnp.broadcast_to(pos_ref[:tm // 2, :].astype(F32), (tm // 2, LANES))
    pos_bot = jnp.broadcast_to(pos_ref[tm // 2:, :].astype(F32), (tm // 2, LANES))
    ang = jnp.where(hl < 64, pos_top, pos_bot) * invf_ref[...]
    cos, sin = jnp.cos(ang), jnp.sin(ang)
    sin = jnp.where((hl & 32) == 0, -sin, sin)
    cos_sw, sin_sw = pltpu.roll(cos, 64, axis=1), pltpu.roll(sin, 64, axis=1)
    rot = jnp.concatenate([jnp.where(hl < 64, cos, sin_sw), jnp.where(hl < 64, cos_sw, sin)], axis=0)

    def rope(tile):
        p = tile * rot
        return p + pltpu.roll(p, 64, axis=1)

    qn = _rms(small[:, :Q_LORA], gq_ref[...]).astype(BF16)
    q_all = _dot(qn, wuq_ref[...])
    for hd in range(MLA_HEADS):
        c0 = hd * QK_WIDTH
        q_ref[:, c0:c0 + LANES] = q_all[:, c0:c0 + LANES].astype(BF16)
        q_ref[:, c0 + LANES:c0 + 2 * LANES] = rope(q_all[:, c0 + LANES:c0 + 2 * LANES]).astype(BF16)

    kvn = _rms(small[:, Q_LORA:Q_LORA + KV_LORA], gkv_ref[...]).astype(BF16)
    kv = _dot(kvn, wukv_ref[...])
    kpe_col = Q_LORA + KV_LORA
    kpe = jnp.where(lane < ROPE_DIM, rope(small[:, kpe_col:kpe_col + LANES]), 0.0).astype(BF16)
    for hd in range(MLA_HEADS):
        c0 = hd * QK_WIDTH
        k_ref[:, c0:c0 + LANES] = kv[:, hd * NOPE_DIM:(hd + 1) * NOPE_DIM].astype(BF16)
        k_ref[:, c0 + LANES:c0 + 2 * LANES] = kpe
    v_ref[...] = kv[:, MLA_HEADS * NOPE_DIM:].astype(BF16)

    @pl.when(pl.program_id(0) % blocks_per_seq == 0)
    def _():
        carry_sc[...] = jnp.zeros_like(carry_sc)

    z = small[:, kpe_col + LANES:kpe_col + 2 * LANES] + bf_ref[...]
    log_f = jnp.minimum(z, 0.0) - jnp.log1p(jnp.exp(-jnp.abs(z)))
    c_blk = _cumsum_rows(log_f) + carry_sc[0:1, :]
    carry_sc[...] = jnp.broadcast_to(c_blk[tm - 1:tm, :], carry_sc.shape)
    c2 = c_blk * LOG2E

    pieces, rest = [], c2
    for _ in range(DECAY_PIECES):
        part = rest.astype(BF16)
        pieces.append(part)
        rest = rest - part.astype(F32)
    kx_ref[...] = _dot(jnp.concatenate(pieces, axis=-1), sel_ref[...]).astype(BF16)


def _mla_prep(x2, pos2, g_pre, w_small, gq, w_uq, gkv, w_ukv, b_forget_pad, invf, seq, tm):
    n_tok = x2.shape[0]
    row = lambda i: (i, 0)
    sel = np.zeros((DECAY_PIECES * LANES, FOX_HEADS * LANES), np.float32)
    for piece in range(DECAY_PIECES):
        for hd in range(FOX_HEADS):
            sel[piece * LANES + hd, hd * LANES + piece] = -1.0
    return pl.pallas_call(
        functools.partial(_mla_prep_kernel, blocks_per_seq=seq // tm),
        grid=(n_tok // tm,),
        in_specs=[pl.BlockSpec((tm, D_MODEL), row),
                  pl.BlockSpec((tm, 1), row),
                  _resident((1, D_MODEL)),
                  _resident((D_MODEL, SMALL_COLS)),
                  _resident((1, Q_LORA)),
                  _resident((Q_LORA, MLA_HEADS * QK_WIDTH)),
                  _resident((1, KV_LORA)),
                  _resident((KV_LORA, MLA_HEADS * (NOPE_DIM + V_DIM))),
                  _resident((1, LANES)),
                  _resident((1, LANES)),
                  _resident((DECAY_PIECES * LANES, FOX_HEADS * LANES))],
        out_specs=[pl.BlockSpec((tm, MLA_HEADS * QK_WIDTH), row),
                   pl.BlockSpec((tm, MLA_HEADS * QK_WIDTH), row),
                   pl.BlockSpec((tm, MLA_HEADS * V_DIM), row),
                   pl.BlockSpec((tm, FOX_HEADS * LANES), row)],
        out_shape=[jax.ShapeDtypeStruct((n_tok, MLA_HEADS * QK_WIDTH), BF16),
                   jax.ShapeDtypeStruct((n_tok, MLA_HEADS * QK_WIDTH), BF16),
                   jax.ShapeDtypeStruct((n_tok, MLA_HEADS * V_DIM), BF16),
                   jax.ShapeDtypeStruct((n_tok, FOX_HEADS * LANES), BF16)],
        scratch_shapes=[pltpu.VMEM((SUBLANES, LANES), F32)],
        compiler_params=_params(1),
        name="mla_prep",
    )(x2, pos2, g_pre, w_small, gq, w_uq, gkv, w_ukv, b_forget_pad, invf, jnp.asarray(sel, BF16))


def _attn_kernel(*refs, blk, unit, use_decay):
    if use_decay:
        q_ref, k_ref, kx_ref, v_ref, o_ref = refs
    else:
        q_ref, k_ref, v_ref, o_ref = refs
        kx_ref = None
    seq = q_ref.shape[1]
    dv = v_ref.shape[2]
    half = blk // 2

    def visible(n_rows, n_cols, row0):
        row = lax.broadcasted_iota(jnp.int32, (n_rows, n_cols), 0) + row0
        col = lax.broadcasted_iota(jnp.int32, (n_rows, n_cols), 1)
        return (col // unit) <= (row // unit)

    mask_a = visible(half, half, 0)
    mask_b = visible(half, blk, half)
    lane = lax.broadcasted_iota(jnp.int32, (half, LANES), 1)
    ones_tile = jnp.where(lane < DECAY_PIECES, 1.0, 0.0).astype(BF16)

    items = []
    for qi in range(seq // blk):
        for ki in range(qi):
            items += [(qi * blk, ki * blk, blk, None), (qi * blk + half, ki * blk, blk, None)]
        items += [(qi * blk, qi * blk, half, mask_a), (qi * blk + half, qi * blk, blk, mask_b)]

    queries = {}

    def scores(item):
        q0, k0, size, _ = item
        if q0 not in queries:
            q = q_ref[0, q0:q0 + half, :]
            queries[q0] = jnp.concatenate([q, ones_tile], axis=-1) if use_decay else q
        k = k_ref[0, k0:k0 + size, :]
        if use_decay:
            k = jnp.concatenate([k, kx_ref[0, k0:k0 + size, :]], axis=-1)
        return lax.dot_general(queries[q0], k, (((1,), (1,)), ((), ())),
                               preferred_element_type=F32)

    def update(s, item, carry):
        _, k0, size, mask = item
        m, l, acc = carry
        if mask is not None:
            s = jnp.where(mask, s, NEG_INF)
        m_new = jnp.maximum(m, jnp.max(s, axis=-1, keepdims=True))
        alpha = jnp.exp2(m - m_new)
        p = jnp.exp2(s - m_new)
        l = alpha * l + jnp.sum(p, axis=-1, keepdims=True)
        acc = alpha * acc + _dot(p.astype(BF16), v_ref[0, k0:k0 + size, :])
        return m_new, l, acc

    ahead = 2
    pending = [scores(item) for item in items[:ahead]]
    carries = {}
    for t, item in enumerate(items):
        if t + ahead < len(items):
            pending.append(scores(items[t + ahead]))
        q0 = item[0]
        carry = carries.get(q0) or (jnp.full((half, 1), NEG_INF, F32), jnp.zeros((half, 1), F32),
                                    jnp.zeros((half, dv), F32))
        carries[q0] = update(pending.pop(0), item, carry)
        if item[3] is not None:
            _, l, acc = carries.pop(q0)
            o_ref[0, q0:q0 + half, :] = (acc / l).astype(o_ref.dtype)


def _attention(q_arr, k_arr, kx_arr, v_arr, *, dq, dk, q_blk0, k_blk0, v_blk0, heads, unit, blk, name):
    batch, seq, _ = q_arr.shape
    in_specs = [pl.BlockSpec((1, seq, dq), lambda b, h: (b, 0, q_blk0 + h)),
                pl.BlockSpec((1, seq, dk), lambda b, h: (b, 0, k_blk0 + h))]
    args = [q_arr, k_arr]
    if kx_arr is not None:
        in_specs.append(pl.BlockSpec((1, seq, LANES), lambda b, h: (b, 0, h)))
        args.append(kx_arr)
    in_specs.append(pl.BlockSpec((1, seq, V_DIM), lambda b, h: (b, 0, v_blk0 + h)))
    args.append(v_arr)
    return pl.pallas_call(
        functools.partial(_attn_kernel, blk=blk, unit=unit, use_decay=kx_arr is not None),
        grid=(batch, heads),
        in_specs=in_specs,
        out_specs=pl.BlockSpec((1, seq, V_DIM), lambda b, h: (b, 0, h)),
        out_shape=jax.ShapeDtypeStruct((batch, seq, heads * V_DIM), BF16),
        compiler_params=_params(2),
        name=name,
    )(*args)


def _merge_kernel(x_ref, om_ref, of_ref, gm_ref, gf_ref, bg_ref, wm_ref, wf_ref, wo_ref, gn_ref,
                  o_ref, *, sub):
    def branches(r):
        rs = slice(r * sub, (r + 1) * sub)
        return _dot(om_ref[rs, :], wm_ref[...]), _dot(of_ref[rs, :], wf_ref[...])

    n_sub = x_ref.shape[0] // sub
    cur = branches(0)
    for r in range(n_sub):
        rs = slice(r * sub, (r + 1) * sub)
        nxt = branches(r + 1) if r + 1 < n_sub else None
        g_mla = jax.nn.sigmoid(gm_ref[rs, :].astype(F32) + bg_ref[:, :D_MODEL])
        g_fox = jax.nn.sigmoid(gf_ref[rs, :].astype(F32) + bg_ref[:, D_MODEL:])
        merged = g_mla * cur[0] + g_fox * cur[1]
        y = _dot(merged.astype(BF16), wo_ref[...])
        o_ref[rs, :] = x_ref[rs, :] + _rms(y, gn_ref[...])
        cur = nxt


def _merge(x2, o_mla, o_fox, proj, b_gate, w_bm, w_bf, w_out, g_post, tm, sub):
    n_tok = x2.shape[0]
    hv = MLA_HEADS * V_DIM
    row = lambda i: (i, 0)
    return pl.pallas_call(
        functools.partial(_merge_kernel, sub=sub),
        grid=(n_tok // tm,),
        in_specs=[pl.BlockSpec((tm, D_MODEL), row),
                  pl.BlockSpec((tm, hv), row),
                  pl.BlockSpec((tm, hv), row),
                  pl.BlockSpec((tm, D_MODEL), lambda i: (i, 0)),
                  pl.BlockSpec((tm, D_MODEL), lambda i: (i, 1)),
                  _resident((1, 2 * D_MODEL)),
                  _resident((hv, D_MODEL)),
                  _resident((hv, D_MODEL)),
                  _resident((D_MODEL, D_MODEL)),
                  _resident((1, D_MODEL))],
        out_specs=pl.BlockSpec((tm, D_MODEL), row),
        out_shape=jax.ShapeDtypeStruct((n_tok, D_MODEL), F32),
        compiler_params=_params(1),
        name="merge",
    )(x2, o_mla, o_fox, proj, proj, b_gate, w_bm, w_bf, w_out, g_post)


def _ffn_kernel(x_ref, gpre_ref, wu_ref, cw_ref, cb_ref, wd_ref, gpost_ref, o_ref, h_sc, halo, u_sc,
                *, blocks_per_seq, sub):
    i = pl.program_id(0)
    k = pl.program_id(1)
    last_k = pl.num_programs(1) - 1
    tm = x_ref.shape[0]
    tc = wd_ref.shape[0]
    n_sub = tm // sub

    @pl.when(i % blocks_per_seq == 0)
    def _():
        halo[k] = jnp.zeros(halo.shape[1:], F32)

    def step(first, last):
        cw, cb = cw_ref[0], cb_ref[0]
        u_sc[0:SUBLANES, :] = halo[k]

        def up(r):
            rs = slice(r * sub, (r + 1) * sub)
            if first:
                h_sc[rs, :] = _rms(x_ref[rs, :], gpre_ref[...]).astype(BF16)
            u_sc[SUBLANES + r * sub:SUBLANES + (r + 1) * sub, :] = _dot(h_sc[rs, :], wu_ref[0])

        def conv(r):
            taps = [u_sc[SUBLANES + r * sub - back:SUBLANES + (r + 1) * sub - back, :]
                    for back in (2, 1, 0)]
            return cw[0:1, :] * taps[0] + cw[1:2, :] * taps[1] + cw[2:3, :] * taps[2] + cb

        up(0)
        for r in range(n_sub):
            rs = slice(r * sub, (r + 1) * sub)
            if r + 1 < n_sub:
                up(r + 1)
            c = conv(r)
            act = (jax.nn.gelu(c[:, :tc], approximate=True) * c[:, tc:]).astype(BF16)
            acc = _dot(act, wd_ref[...])
            if not first:
                acc = o_ref[rs, :] + acc
            o_ref[rs, :] = x_ref[rs, :] + _rms(acc, gpost_ref[...]) if last else acc
        halo[k] = u_sc[tm:tm + SUBLANES, :]

    pl.when(k == 0)(lambda: step(True, False))
    pl.when((k > 0) & (k < last_k))(lambda: step(False, False))
    pl.when(k == last_k)(lambda: step(False, True))


def _ffn(x1, g_pre, w_up, conv_w, conv_b, w_down, g_post, seq, tm, sub, tc):
    n_tok = x1.shape[0]
    nk = D_FF // tc
    assert nk >= 2, "the first and last d_ff chunks take different code paths"
    chunk = lambda i, k: (k, 0, 0)
    return pl.pallas_call(
        functools.partial(_ffn_kernel, blocks_per_seq=seq // tm, sub=sub),
        grid=(n_tok // tm, nk),
        in_specs=[pl.BlockSpec((tm, D_MODEL), lambda i, k: (i, 0)),
                  pl.BlockSpec((1, D_MODEL), lambda i, k: (0, 0)),
                  pl.BlockSpec((1, D_MODEL, 2 * tc), chunk),
                  pl.BlockSpec((1, CONV_WIDTH, 2 * tc), chunk),
                  pl.BlockSpec((1, 1, 2 * tc), chunk),
                  pl.BlockSpec((tc, D_MODEL), lambda i, k: (k, 0)),
                  pl.BlockSpec((1, D_MODEL), lambda i, k: (0, 0))],
        out_specs=pl.BlockSpec((tm, D_MODEL), lambda i, k: (i, 0)),
        out_shape=jax.ShapeDtypeStruct((n_tok, D_MODEL), F32),
        scratch_shapes=[pltpu.VMEM((tm, D_MODEL), BF16),
                        pltpu.VMEM((nk, SUBLANES, 2 * tc), F32),
                        pltpu.VMEM((SUBLANES + tm, 2 * tc), F32)],
        compiler_params=_params(2),
        name="ffn",
    )(x1, g_pre, w_up, conv_w, conv_b, w_down, g_post)


def _chunk_major(w, tc):
    rows = w.shape[0]
    w = w.reshape(rows, 2, D_FF // tc, tc)
    return jnp.transpose(w, (2, 0, 1, 3)).reshape(D_FF // tc, rows, 2 * tc)


def _swap_halves(w):
    half = w.shape[-1] // 2
    return jnp.concatenate([w[..., half:], w[..., :half]], axis=-1)


def _regroup_in_kernel(w_ref, main_ref, small_ref):
    hd = FOX_HEADS * FOX_HEAD_DIM
    c_kv, c_pe = Q_LORA + KV_LORA, Q_LORA + KV_LORA + ROPE_DIM
    c_fk, c_fl = c_pe + hd, c_pe + 3 * hd
    c_g = c_fl + FOX_HEADS
    w = w_ref[...]
    main_ref[:, :2 * D_MODEL] = w[:, c_g:].astype(BF16)
    main_ref[:, 2 * D_MODEL:2 * D_MODEL + hd] = (
        w[:, c_pe:c_fk] * (LOG2E * FOX_HEAD_DIM ** -0.5)).astype(BF16)
    main_ref[:, 2 * D_MODEL + hd:] = w[:, c_fk:c_fl].astype(BF16)
    pad = jnp.zeros((w.shape[0], SMALL_COLS - (c_pe + ROPE_DIM + FOX_HEADS)), F32)
    small_ref[...] = jnp.concatenate(
        [w[:, :c_pe], _swap_halves(w[:, c_kv:c_pe]), w[:, c_fl:c_g], pad], axis=1).astype(BF16)


def _regroup_in(w_in, rows):
    d_in = w_in.shape[1]
    return pl.pallas_call(
        _regroup_in_kernel,
        grid=(D_MODEL // rows,),
        in_specs=[pl.BlockSpec((rows, d_in), lambda i: (i, 0))],
        out_specs=[pl.BlockSpec((rows, MAIN_COLS), lambda i: (i, 0)),
                   pl.BlockSpec((rows, SMALL_COLS), lambda i: (i, 0))],
        out_shape=[jax.ShapeDtypeStruct((D_MODEL, MAIN_COLS), BF16),
                   jax.ShapeDtypeStruct((D_MODEL, SMALL_COLS), BF16)],
        compiler_params=_params(1),
        name="regroup_w_in",
    )(w_in)


def _chunk_up_kernel(g_ref, v_ref, o_ref):
    tc = g_ref.shape[1]
    o_ref[0, :, :tc] = g_ref[...].astype(BF16)
    o_ref[0, :, tc:] = v_ref[...].astype(BF16)


def _chunk_up(w_up, tc):
    nk = D_FF // tc
    return pl.pallas_call(
        _chunk_up_kernel,
        grid=(nk,),
        in_specs=[pl.BlockSpec((D_MODEL, tc), lambda k: (0, k)),
                  pl.BlockSpec((D_MODEL, tc), lambda k: (0, nk + k))],
        out_specs=pl.BlockSpec((1, D_MODEL, 2 * tc), lambda k: (k, 0, 0)),
        out_shape=jax.ShapeDtypeStruct((nk, D_MODEL, 2 * tc), BF16),
        compiler_params=_params(1),
        name="chunk_w_up",
    )(w_up, w_up)


def _prep_weights(w_in, w_uq, w_ukv):
    w_main, w_small = _regroup_in(w_in, 256)

    uq = w_uq.reshape(Q_LORA, MLA_HEADS, NOPE_DIM + ROPE_DIM) * (LOG2E * (NOPE_DIM + ROPE_DIM) ** -0.5)
    uq = jnp.concatenate([uq, _swap_halves(uq[..., NOPE_DIM:])], axis=-1)
    w_uq2 = uq.reshape(Q_LORA, MLA_HEADS * QK_WIDTH).astype(BF16)
    ukv = w_ukv.reshape(KV_LORA, MLA_HEADS, NOPE_DIM + V_DIM)
    w_ukv2 = jnp.concatenate([ukv[..., :NOPE_DIM].reshape(KV_LORA, -1),
                              ukv[..., NOPE_DIM:].reshape(KV_LORA, -1)], axis=1).astype(BF16)
    return w_small, w_main, w_uq2, w_ukv2


def kernel(x, positions, pre_mix_norm, w_in, q_a_norm, w_uq, kv_a_norm, w_ukv, b_forget, b_gate,
           w_branch_mla, w_branch_fox, w_out, post_mix_norm, pre_ffn_norm, w_up, conv_w, conv_b,
           w_down, post_ffn_norm):
    batch, seq, _ = x.shape
    n_tok = batch * seq
    t = _tiles(seq)

    w_small, w_main, w_uq2, w_ukv2 = _prep_weights(w_in, w_uq, w_ukv)
    x2 = x.reshape(n_tok, D_MODEL)
    pos2 = positions.reshape(n_tok, 1)
    row = lambda v: v.reshape(1, -1).astype(F32)
    inv_freq = 1.0 / (ROPE_THETA ** (jnp.arange(0, ROPE_DIM, 2, dtype=F32) / ROPE_DIM))
    invf = jnp.tile(inv_freq, LANES // inv_freq.shape[0]).reshape(1, LANES)
    bf_pad = jnp.zeros((1, LANES), F32).at[0, :FOX_HEADS].set(b_forget.astype(F32))

    proj = _proj_main(x2, row(pre_mix_norm), w_main, t.proj_rows, t.proj_sub, t.proj_cols)
    q_mla, k_mla, v_mla, kx = _mla_prep(x2, pos2, row(pre_mix_norm), w_small, row(q_a_norm), w_uq2,
                                        row(kv_a_norm), w_ukv2, bf_pad, invf, seq, t.prep_rows)

    per_seq = lambda a: a.reshape(batch, seq, -1)
    o_mla = _attention(per_seq(q_mla), per_seq(k_mla), None, per_seq(v_mla), dq=QK_WIDTH,
                       dk=QK_WIDTH, q_blk0=0, k_blk0=0, v_blk0=0, heads=MLA_HEADS, unit=CHUNK,
                       blk=t.attn_blk, name="attn_mla")
    proj3 = per_seq(proj)
    fq0 = 2 * D_MODEL // FOX_HEAD_DIM
    o_fox = _attention(proj3, proj3, per_seq(kx), proj3, dq=FOX_HEAD_DIM, dk=FOX_HEAD_DIM,
                       q_blk0=fq0, k_blk0=fq0 + FOX_HEADS, v_blk0=fq0 + 2 * FOX_HEADS,
                       heads=FOX_HEADS, unit=1, blk=t.attn_blk, name="attn_fox")

    x1 = _merge(x2, o_mla.reshape(n_tok, -1), o_fox.reshape(n_tok, -1), proj, row(b_gate),
                w_branch_mla.astype(BF16), w_branch_fox.astype(BF16), w_out.astype(BF16),
                row(post_mix_norm), t.merge_rows, t.merge_sub)
    out = _ffn(x1, row(pre_ffn_norm), _chunk_up(w_up, t.ffn_cols),
               _chunk_major(conv_w.astype(F32), t.ffn_cols), _chunk_major(row(conv_b), t.ffn_cols),
               w_down.astype(BF16), row(post_ffn_norm), seq, t.ffn_rows, t.ffn_sub, t.ffn_cols)
    return out.reshape(batch, seq, D_MODEL)
```

```python
import collections
import functools
import math

import jax
import jax.numpy as jnp
import numpy as np
from jax import lax
from jax.experimental import pallas as pl
from jax.experimental.pallas import tpu as pltpu

D_MODEL = 2048
CHUNK = 64
MLA_HEADS = 8
Q_LORA = 512
KV_LORA = 256
NOPE_DIM = 128
ROPE_DIM = 64
V_DIM = 128
ROPE_THETA = 10000.0
FOX_HEADS = 8
FOX_HEAD_DIM = 128
D_FF = 5632
CONV_WIDTH = 3
EPS = 1e-6
NEG_INF = -1e30
LOG2E = math.log2(math.e)

LANES = 128
SUBLANES = 8
QK_WIDTH = 256
SMALL_COLS = 1024
MAIN_COLS = 2 * D_MODEL + 3 * FOX_HEADS * FOX_HEAD_DIM
DECAY_PIECES = 3
VMEM_LIMIT = 56 * 1024 * 1024

F32 = jnp.float32
BF16 = jnp.bfloat16

Tiles = collections.namedtuple(
    "Tiles", "proj_rows proj_sub proj_cols prep_rows attn_blk merge_rows merge_sub ffn_rows ffn_sub ffn_cols")


def _tiles(seq):
    return Tiles(proj_rows=min(1024, seq), proj_sub=min(256, seq), proj_cols=1024, prep_rows=min(512, seq),
                 attn_blk=min(512, seq), merge_rows=min(512, seq), merge_sub=min(256, seq),
                 ffn_rows=min(512, seq), ffn_sub=min(256, seq), ffn_cols=512)


def _rms(x, g):
    return x * lax.rsqrt(jnp.mean(x * x, axis=-1, keepdims=True) + EPS) * g


def _dot(a, b):
    return jnp.dot(a, b, preferred_element_type=F32)


def _dot_t(a, b_t):
    return lax.dot_general(a, b_t, (((1,), (1,)), ((), ())), preferred_element_type=F32)


def _params(n_axes):
    return pltpu.CompilerParams(dimension_semantics=("arbitrary",) * n_axes,
                                vmem_limit_bytes=VMEM_LIMIT)


def _resident(shape):
    return pl.BlockSpec(shape, lambda *_: (0,) * len(shape), pipeline_mode=pl.Buffered(1))


def _proj_main_kernel(x_ref, g_ref, w_ref, o_ref, h_sc, *, sub):
    j = pl.program_id(1)

    @pl.when(j == 0)
    def _():
        for r in range(x_ref.shape[0] // sub):
            rs = slice(r * sub, (r + 1) * sub)
            h_sc[rs, :] = _rms(x_ref[rs, :], g_ref[...]).astype(BF16)
            o_ref[rs, :] = _dot_t(h_sc[rs, :], w_ref[...]).astype(o_ref.dtype)

    @pl.when(j > 0)
    def _():
        o_ref[...] = _dot_t(h_sc[...], w_ref[...]).astype(o_ref.dtype)


def _proj_main(x2, g_pre, w_main, tm, sub, tn):
    n_tok = x2.shape[0]
    return pl.pallas_call(
        functools.partial(_proj_main_kernel, sub=sub),
        grid=(n_tok // tm, MAIN_COLS // tn),
        in_specs=[pl.BlockSpec((tm, D_MODEL), lambda i, j: (i, 0)),
                  pl.BlockSpec((1, D_MODEL), lambda i, j: (0, 0)),
                  pl.BlockSpec((tn, D_MODEL), lambda i, j: (j, 0))],
        out_specs=pl.BlockSpec((tm, tn), lambda i, j: (i, j)),
        out_shape=jax.ShapeDtypeStruct((n_tok, MAIN_COLS), BF16),
        scratch_shapes=[pltpu.VMEM((tm, D_MODEL), BF16)],
        compiler_params=_params(2),
        name="proj_main",
    )(x2, g_pre, w_main)


def _cumsum_rows(v):
    n = v.shape[0]
    rows = lax.broadcasted_iota(jnp.int32, v.shape, 0)
    shift = 1
    while shift < n:
        v = v + jnp.where(rows >= shift, pltpu.roll(v, shift, axis=0), 0.0)
        shift *= 2
    return v


def _mla_prep_kernel(x_ref, pos_ref, g_ref, ws_ref, gq_ref, wuq_ref, gkv_ref, wukv_ref, bf_ref,
                     invf_ref, sel_ref, q_ref, k_ref, v_ref, kx_ref, carry_sc, *, blocks_per_seq):
    tm = x_ref.shape[0]
    h = _rms(x_ref[...], g_ref[...]).astype(BF16)
    small = _dot_t(h, ws_ref[...])

    lane = lax.broadcasted_iota(jnp.int32, (tm, LANES), 1)
    hl = lax.broadcasted_iota(jnp.int32, (tm // 2, LANES), 1)
    pos_top = jnp.broadcast_to(pos_ref[:tm // 2, :].astype(F32), (tm // 2, LANES))
    pos_bot = jnp.broadcast_to(pos_ref[tm // 2:, :].astype(F32), (tm // 2, LANES))
    ang = jnp.where(hl < 64, pos_top, pos_bot) * invf_ref[...]
    cos, sin = jnp.cos(ang), jnp.sin(ang)
    sin = jnp.where((hl & 32) == 0, -sin, sin)
    cos_sw, sin_sw = pltpu.roll(cos, 64, axis=1), pltpu.roll(sin, 64, axis=1)
    rot = jnp.concatenate([jnp.where(hl < 64, cos, sin_sw), jnp.where(hl < 64, cos_sw, sin)], axis=0)

    def rope(tile):
        p = tile * rot
        return p + pltpu.roll(p, 64, axis=1)

    qn = _rms(small[:, :Q_LORA], gq_ref[...]).astype(BF16)
    q_all = _dot(qn, wuq_ref[...])
    for hd in range(MLA_HEADS):
        c0 = hd * QK_WIDTH
        q_ref[:, c0:c0 + LANES] = q_all[:, c0:c0 + LANES].astype(BF16)
        q_ref[:, c0 + LANES:c0 + 2 * LANES] = rope(q_all[:, c0 + LANES:c0 + 2 * LANES]).astype(BF16)

    kvn = _rms(small[:, Q_LORA:Q_LORA + KV_LORA], gkv_ref[...]).astype(BF16)
    kv = _dot(kvn, wukv_ref[...])
    kpe_col = Q_LORA + KV_LORA
    kpe = jnp.where(lane < ROPE_DIM, rope(small[:, kpe_col:kpe_col + LANES]), 0.0).astype(BF16)
    for hd in range(MLA_HEADS):
        c0 = hd * QK_WIDTH
        k_ref[:, c0:c0 + LANES] = kv[:, hd * NOPE_DIM:(hd + 1) * NOPE_DIM].astype(BF16)
        k_ref[:, c0 + LANES:c0 + 2 * LANES] = kpe
    v_ref[...] = kv[:, MLA_HEADS * NOPE_DIM:].astype(BF16)

    @pl.when(pl.program_id(0) % blocks_per_seq == 0)
    def _():
        carry_sc[...] = jnp.zeros_like(carry_sc)

    z = small[:, kpe_col + LANES:kpe_col + 2 * LANES] + bf_ref[...]
    log_f = jnp.minimum(z, 0.0) - jnp.log1p(jnp.exp(-jnp.abs(z)))
    c_blk = _cumsum_rows(log_f) + carry_sc[0:1, :]
    carry_sc[...] = jnp.broadcast_to(c_blk[tm - 1:tm, :], carry_sc.shape)
    c2 = c_blk * LOG2E

    pieces, rest = [], c2
    for _ in range(DECAY_PIECES):
        part = rest.astype(BF16)
        pieces.append(part)
        rest = rest - part.astype(F32)
    kx_ref[...] = _dot(jnp.concatenate(pieces, axis=-1), sel_ref[...]).astype(BF16)


def _mla_prep(x2, pos2, g_pre, w_small, gq, w_uq, gkv, w_ukv, b_forget_pad, invf, seq, tm):
    n_tok = x2.shape[0]
    row = lambda i: (i, 0)
    sel = np.zeros((DECAY_PIECES * LANES, FOX_HEADS * LANES), np.float32)
    for piece in range(DECAY_PIECES):
        for hd in range(FOX_HEADS):
            sel[piece * LANES + hd, hd * LANES + piece] = -1.0
    return pl.pallas_call(
        functools.partial(_mla_prep_kernel, blocks_per_seq=seq // tm),
        grid=(n_tok // tm,),
        in_specs=[pl.BlockSpec((tm, D_MODEL), row),
                  pl.BlockSpec((tm, 1), row),
                  _resident((1, D_MODEL)),
                  _resident((SMALL_COLS, D_MODEL)),
                  _resident((1, Q_LORA)),
                  _resident((Q_LORA, MLA_HEADS * QK_WIDTH)),
                  _resident((1, KV_LORA)),
                  _resident((KV_LORA, MLA_HEADS * (NOPE_DIM + V_DIM))),
                  _resident((1, LANES)),
                  _resident((1, LANES)),
                  _resident((DECAY_PIECES * LANES, FOX_HEADS * LANES))],
        out_specs=[pl.BlockSpec((tm, MLA_HEADS * QK_WIDTH), row),
                   pl.BlockSpec((tm, MLA_HEADS * QK_WIDTH), row),
                   pl.BlockSpec((tm, MLA_HEADS * V_DIM), row),
                   pl.BlockSpec((tm, FOX_HEADS * LANES), row)],
        out_shape=[jax.ShapeDtypeStruct((n_tok, MLA_HEADS * QK_WIDTH), BF16),
                   jax.ShapeDtypeStruct((n_tok, MLA_HEADS * QK_WIDTH), BF16),
                   jax.ShapeDtypeStruct((n_tok, MLA_HEADS * V_DIM), BF16),
                   jax.ShapeDtypeStruct((n_tok, FOX_HEADS * LANES), BF16)],
        scratch_shapes=[pltpu.VMEM((SUBLANES, LANES), F32)],
        compiler_params=_params(1),
        name="mla_prep",
    )(x2, pos2, g_pre, w_small, gq, w_uq, gkv, w_ukv, b_forget_pad, invf, jnp.asarray(sel, BF16))


def _attn_kernel(*refs, blk, unit, use_decay):
    if use_decay:
        q_ref, k_ref, kx_ref, v_ref, o_ref = refs
    else:
        q_ref, k_ref, v_ref, o_ref = refs
        kx_ref = None
    seq = q_ref.shape[1]
    dv = v_ref.shape[2]
    half = blk // 2

    def visible(n_rows, n_cols, row0):
        row = lax.broadcasted_iota(jnp.int32, (n_rows, n_cols), 0) + row0
        col = lax.broadcasted_iota(jnp.int32, (n_rows, n_cols), 1)
        return (col // unit) <= (row // unit)

    mask_a = visible(half, half, 0)
    mask_b = visible(half, blk, half)
    lane = lax.broadcasted_iota(jnp.int32, (half, LANES), 1)
    ones_tile = jnp.where(lane < DECAY_PIECES, 1.0, 0.0).astype(BF16)

    items = []
    for qi in range(seq // blk):
        for ki in range(qi):
            items += [(qi * blk, ki * blk, blk, None), (qi * blk + half, ki * blk, blk, None)]
        items += [(qi * blk, qi * blk, half, mask_a), (qi * blk + half, qi * blk, blk, mask_b)]

    queries = {}

    def scores(item):
        q0, k0, size, _ = item
        if q0 not in queries:
            q = q_ref[0, q0:q0 + half, :]
            queries[q0] = jnp.concatenate([q, ones_tile], axis=-1) if use_decay else q
        k = k_ref[0, k0:k0 + size, :]
        if use_decay:
            k = jnp.concatenate([k, kx_ref[0, k0:k0 + size, :]], axis=-1)
        return lax.dot_general(queries[q0], k, (((1,), (1,)), ((), ())),
                               preferred_element_type=F32)

    def update(s, item, carry):
        _, k0, size, mask = item
        m, l, acc = carry
        if mask is not None:
            s = jnp.where(mask, s, NEG_INF)
        m_new = jnp.maximum(m, jnp.max(s, axis=-1, keepdims=True))
        alpha = jnp.exp2(m - m_new)
        p = jnp.exp2(s - m_new)
        l = alpha * l + jnp.sum(p, axis=-1, keepdims=True)
        acc = alpha * acc + _dot(p.astype(BF16), v_ref[0, k0:k0 + size, :])
        return m_new, l, acc

    ahead = 2
    pending = [scores(item) for item in items[:ahead]]
    carries = {}
    for t, item in enumerate(items):
        if t + ahead < len(items):
            pending.append(scores(items[t + ahead]))
        q0 = item[0]
        carry = carries.get(q0) or (jnp.full((half, 1), NEG_INF, F32), jnp.zeros((half, 1), F32),
                                    jnp.zeros((half, dv), F32))
        carries[q0] = update(pending.pop(0), item, carry)
        if item[3] is not None:
            _, l, acc = carries.pop(q0)
            o_ref[0, q0:q0 + half, :] = (acc / l).astype(o_ref.dtype)


def _attention(q_arr, k_arr, kx_arr, v_arr, *, dq, dk, q_blk0, k_blk0, v_blk0, heads, unit, blk, name):
    batch, seq, _ = q_arr.shape
    in_specs = [pl.BlockSpec((1, seq, dq), lambda b, h: (b, 0, q_blk0 + h)),
                pl.BlockSpec((1, seq, dk), lambda b, h: (b, 0, k_blk0 + h))]
    args = [q_arr, k_arr]
    if kx_arr is not None:
        in_specs.append(pl.BlockSpec((1, seq, LANES), lambda b, h: (b, 0, h)))
        args.append(kx_arr)
    in_specs.append(pl.BlockSpec((1, seq, V_DIM), lambda b, h: (b, 0, v_blk0 + h)))
    args.append(v_arr)
    return pl.pallas_call(
        functools.partial(_attn_kernel, blk=blk, unit=unit, use_decay=kx_arr is not None),
        grid=(batch, heads),
        in_specs=in_specs,
        out_specs=pl.BlockSpec((1, seq, V_DIM), lambda b, h: (b, 0, h)),
        out_shape=jax.ShapeDtypeStruct((batch, seq, heads * V_DIM), BF16),
        compiler_params=_params(2),
        name=name,
    )(*args)


def _merge_kernel(x_ref, om_ref, of_ref, gm_ref, gf_ref, bg_ref, wm_ref, wf_ref, wo_ref, gn_ref,
                  o_ref, *, sub):
    def branches(r):
        rs = slice(r * sub, (r + 1) * sub)
        return _dot(om_ref[rs, :], wm_ref[...]), _dot(of_ref[rs, :], wf_ref[...])

    n_sub = x_ref.shape[0] // sub
    cur = branches(0)
    for r in range(n_sub):
        rs = slice(r * sub, (r + 1) * sub)
        nxt = branches(r + 1) if r + 1 < n_sub else None
        g_mla = jax.nn.sigmoid(gm_ref[rs, :].astype(F32) + bg_ref[:, :D_MODEL])
        g_fox = jax.nn.sigmoid(gf_ref[rs, :].astype(F32) + bg_ref[:, D_MODEL:])
        merged = g_mla * cur[0] + g_fox * cur[1]
        y = _dot(merged.astype(BF16), wo_ref[...])
        o_ref[rs, :] = x_ref[rs, :] + _rms(y, gn_ref[...])
        cur = nxt


def _merge(x2, o_mla, o_fox, proj, b_gate, w_bm, w_bf, w_out, g_post, tm, sub):
    n_tok = x2.shape[0]
    hv = MLA_HEADS * V_DIM
    row = lambda i: (i, 0)
    return pl.pallas_call(
        functools.partial(_merge_kernel, sub=sub),
        grid=(n_tok // tm,),
        in_specs=[pl.BlockSpec((tm, D_MODEL), row),
                  pl.BlockSpec((tm, hv), row),
                  pl.BlockSpec((tm, hv), row),
                  pl.BlockSpec((tm, D_MODEL), lambda i: (i, 0)),
                  pl.BlockSpec((tm, D_MODEL), lambda i: (i, 1)),
                  _resident((1, 2 * D_MODEL)),
                  _resident((hv, D_MODEL)),
                  _resident((hv, D_MODEL)),
                  _resident((D_MODEL, D_MODEL)),
                  _resident((1, D_MODEL))],
        out_specs=pl.BlockSpec((tm, D_MODEL), row),
        out_shape=jax.ShapeDtypeStruct((n_tok, D_MODEL), F32),
        compiler_params=_params(1),
        name="merge",
    )(x2, o_mla, o_fox, proj, proj, b_gate, w_bm, w_bf, w_out, g_post)


def _ffn_kernel(x_ref, gpre_ref, wu_ref, cw_ref, cb_ref, wd_ref, gpost_ref, o_ref, h_sc, halo, u_sc,
                *, blocks_per_seq, sub):
    i = pl.program_id(0)
    k = pl.program_id(1)
    last_k = pl.num_programs(1) - 1
    tm = x_ref.shape[0]
    tc = wd_ref.shape[0]
    n_sub = tm // sub

    @pl.when(i % blocks_per_seq == 0)
    def _():
        halo[k] = jnp.zeros(halo.shape[1:], F32)

    def step(first, last):
        cw, cb = cw_ref[0], cb_ref[0]
        u_sc[0:SUBLANES, :] = halo[k]

        def up(r):
            rs = slice(r * sub, (r + 1) * sub)
            if first:
                h_sc[rs, :] = _rms(x_ref[rs, :], gpre_ref[...]).astype(BF16)
            u_sc[SUBLANES + r * sub:SUBLANES + (r + 1) * sub, :] = _dot(h_sc[rs, :], wu_ref[0])

        def conv(r):
            taps = [u_sc[SUBLANES + r * sub - back:SUBLANES + (r + 1) * sub - back, :]
                    for back in (2, 1, 0)]
            return cw[0:1, :] * taps[0] + cw[1:2, :] * taps[1] + cw[2:3, :] * taps[2] + cb

        up(0)
        for r in range(n_sub):
            rs = slice(r * sub, (r + 1) * sub)
            if r + 1 < n_sub:
                up(r + 1)
            c = conv(r)
            act = (jax.nn.gelu(c[:, :tc], approximate=True) * c[:, tc:]).astype(BF16)
            acc = _dot(act, wd_ref[...])
            if not first:
                acc = o_ref[rs, :] + acc
            o_ref[rs, :] = x_ref[rs, :] + _rms(acc, gpost_ref[...]) if last else acc
        halo[k] = u_sc[tm:tm + SUBLANES, :]

    pl.when(k == 0)(lambda: step(True, False))
    pl.when((k > 0) & (k < last_k))(lambda: step(False, False))
    pl.when(k == last_k)(lambda: step(False, True))


def _ffn(x1, g_pre, w_up, conv_w, conv_b, w_down, g_post, seq, tm, sub, tc):
    n_tok = x1.shape[0]
    nk = D_FF // tc
    assert nk >= 2, "the first and last d_ff chunks take different code paths"
    chunk = lambda i, k: (k, 0, 0)
    return pl.pallas_call(
        functools.partial(_ffn_kernel, blocks_per_seq=seq // tm, sub=sub),
        grid=(n_tok // tm, nk),
        in_specs=[pl.BlockSpec((tm, D_MODEL), lambda i, k: (i, 0)),
                  pl.BlockSpec((1, D_MODEL), lambda i, k: (0, 0)),
                  pl.BlockSpec((1, D_MODEL, 2 * tc), chunk),
                  pl.BlockSpec((1, CONV_WIDTH, 2 * tc), chunk),
                  pl.BlockSpec((1, 1, 2 * tc), chunk),
                  pl.BlockSpec((tc, D_MODEL), lambda i, k: (k, 0)),
                  pl.BlockSpec((1, D_MODEL), lambda i, k: (0, 0))],
        out_specs=pl.BlockSpec((tm, D_MODEL), lambda i, k: (i, 0)),
        out_shape=jax.ShapeDtypeStruct((n_tok, D_MODEL), F32),
        scratch_shapes=[pltpu.VMEM((tm, D_MODEL), BF16),
                        pltpu.VMEM((nk, SUBLANES, 2 * tc), F32),
                        pltpu.VMEM((SUBLANES + tm, 2 * tc), F32)],
        compiler_params=_params(2),
        name="ffn",
    )(x1, g_pre, w_up, conv_w, conv_b, w_down, g_post)


def _chunk_major(w, tc):
    rows = w.shape[0]
    w = w.reshape(rows, 2, D_FF // tc, tc)
    return jnp.transpose(w, (2, 0, 1, 3)).reshape(D_FF // tc, rows, 2 * tc)


def _swap_halves(w):
    half = w.shape[-1] // 2
    return jnp.concatenate([w[..., half:], w[..., :half]], axis=-1)


def _regroup_in_kernel(w_ref, main_ref, small_ref):
    hd = FOX_HEADS * FOX_HEAD_DIM
    c_kv, c_pe = Q_LORA + KV_LORA, Q_LORA + KV_LORA + ROPE_DIM
    c_fk, c_fl = c_pe + hd, c_pe + 3 * hd
    c_g = c_fl + FOX_HEADS
    half = ROPE_DIM // 2
    cols = w_ref.shape[1]
    main_ref[:2 * D_MODEL, :] = w_ref[c_g:, :].astype(BF16)
    main_ref[2 * D_MODEL:2 * D_MODEL + hd, :] = (
        w_ref[c_pe:c_fk, :] * (LOG2E * FOX_HEAD_DIM ** -0.5)).astype(BF16)
    main_ref[2 * D_MODEL + hd:, :] = w_ref[c_fk:c_fl, :].astype(BF16)
    small_ref[:c_pe, :] = w_ref[:c_pe, :].astype(BF16)
    small_ref[c_pe:c_pe + ROPE_DIM, :] = jnp.concatenate(
        [w_ref[c_kv + half:c_pe, :], w_ref[c_kv:c_kv + half, :]], axis=0).astype(BF16)
    tail = SMALL_COLS - (c_pe + ROPE_DIM)
    small_ref[c_pe + ROPE_DIM:, :] = jnp.concatenate(
        [w_ref[c_fl:c_g, :], jnp.zeros((tail - FOX_HEADS, cols), F32)], axis=0).astype(BF16)


def _regroup_in(w_in_t, cols):
    d_in = w_in_t.shape[0]
    return pl.pallas_call(
        _regroup_in_kernel,
        grid=(D_MODEL // cols,),
        in_specs=[pl.BlockSpec((d_in, cols), lambda i: (0, i))],
        out_specs=[pl.BlockSpec((MAIN_COLS, cols), lambda i: (0, i)),
                   pl.BlockSpec((SMALL_COLS, cols), lambda i: (0, i))],
        out_shape=[jax.ShapeDtypeStruct((MAIN_COLS, D_MODEL), BF16),
                   jax.ShapeDtypeStruct((SMALL_COLS, D_MODEL), BF16)],
        compiler_params=_params(1),
        name="regroup_w_in",
    )(w_in_t)


def _chunk_up_kernel(g_ref, v_ref, o_ref):
    tc = g_ref.shape[1]
    o_ref[0, :, :tc] = g_ref[...].astype(BF16)
    o_ref[0, :, tc:] = v_ref[...].astype(BF16)


def _chunk_up(w_up, tc):
    nk = D_FF // tc
    return pl.pallas_call(
        _chunk_up_kernel,
        grid=(nk,),
        in_specs=[pl.BlockSpec((D_MODEL, tc), lambda k: (0, k)),
                  pl.BlockSpec((D_MODEL, tc), lambda k: (0, nk + k))],
        out_specs=pl.BlockSpec((1, D_MODEL, 2 * tc), lambda k: (k, 0, 0)),
        out_shape=jax.ShapeDtypeStruct((nk, D_MODEL, 2 * tc), BF16),
        compiler_params=_params(1),
        name="chunk_w_up",
    )(w_up, w_up)


def _prep_weights(w_in, w_uq, w_ukv):
    w_main, w_small = _regroup_in(w_in.T, 256)

    uq = w_uq.reshape(Q_LORA, MLA_HEADS, NOPE_DIM + ROPE_DIM) * (LOG2E * (NOPE_DIM + ROPE_DIM) ** -0.5)
    uq = jnp.concatenate([uq, _swap_halves(uq[..., NOPE_DIM:])], axis=-1)
    w_uq2 = uq.reshape(Q_LORA, MLA_HEADS * QK_WIDTH).astype(BF16)
    ukv = w_ukv.reshape(KV_LORA, MLA_HEADS, NOPE_DIM + V_DIM)
    w_ukv2 = jnp.concatenate([ukv[..., :NOPE_DIM].reshape(KV_LORA, -1),
                              ukv[..., NOPE_DIM:].reshape(KV_LORA, -1)], axis=1).astype(BF16)
    return w_small, w_main, w_uq2, w_ukv2


def kernel(x, positions, pre_mix_norm, w_in, q_a_norm, w_uq, kv_a_norm, w_ukv, b_forget, b_gate,
           w_branch_mla, w_branch_fox, w_out, post_mix_norm, pre_ffn_norm, w_up, conv_w, conv_b,
           w_down, post_ffn_norm):
    batch, seq, _ = x.shape
    n_tok = batch * seq
    t = _tiles(seq)

    w_small, w_main, w_uq2, w_ukv2 = _prep_weights(w_in, w_uq, w_ukv)
    x2 = x.reshape(n_tok, D_MODEL)
    pos2 = positions.reshape(n_tok, 1)
    row = lambda v: v.reshape(1, -1).astype(F32)
    inv_freq = 1.0 / (ROPE_THETA ** (jnp.arange(0, ROPE_DIM, 2, dtype=F32) / ROPE_DIM))
    invf = jnp.tile(inv_freq, LANES // inv_freq.shape[0]).reshape(1, LANES)
    bf_pad = jnp.zeros((1, LANES), F32).at[0, :FOX_HEADS].set(b_forget.astype(F32))

    proj = _proj_main(x2, row(pre_mix_norm), w_main, t.proj_rows, t.proj_sub, t.proj_cols)
    q_mla, k_mla, v_mla, kx = _mla_prep(x2, pos2, row(pre_mix_norm), w_small, row(q_a_norm), w_uq2,
                                        row(kv_a_norm), w_ukv2, bf_pad, invf, seq, t.prep_rows)

    per_seq = lambda a: a.reshape(batch, seq, -1)
    o_mla = _attention(per_seq(q_mla), per_seq(k_mla), None, per_seq(v_mla), dq=QK_WIDTH,
                       dk=QK_WIDTH, q_blk0=0, k_blk0=0, v_blk0=0, heads=MLA_HEADS, unit=CHUNK,
                       blk=t.attn_blk, name="attn_mla")
    proj3 = per_seq(proj)
    fq0 = 2 * D_MODEL // FOX_HEAD_DIM
    o_fox = _attention(proj3, proj3, per_seq(kx), proj3, dq=FOX_HEAD_DIM, dk=FOX_HEAD_DIM,
                       q_blk0=fq0, k_blk0=fq0 + FOX_HEADS, v_blk0=fq0 + 2 * FOX_HEADS,
                       heads=FOX_HEADS, unit=1, blk=t.attn_blk, name="attn_fox")

    x1 = _merge(x2, o_mla.reshape(n_tok, -1), o_fox.reshape(n_tok, -1), proj, row(b_gate),
                w_branch_mla.astype(BF16), w_branch_fox.astype(BF16), w_out.astype(BF16),
                row(post_mix_norm), t.merge_rows, t.merge_sub)
    out = _ffn(x1, row(pre_ffn_norm), _chunk_up(w_up, t.ffn_cols),
               _chunk_major(conv_w.astype(F32), t.ffn_cols), _chunk_major(row(conv_b), t.ffn_cols),
               w_down.astype(BF16), row(post_ffn_norm), seq, t.ffn_rows, t.ffn_sub, t.ffn_cols)
    return out.reshape(batch, seq, D_MODEL)
```
